```python
import math
import jax
import jax.numpy as jnp
from jax import lax
import numpy as np

D_MODEL = 2048
BATCH = 4
SEQ = 2048
DEPTH = 2
DEC_BATCH = 128
DEC_SEQ = 8
PAST_LEN = 2048
PAGE_SIZE = 128

GLA_HEADS = 4
GLA_DK = D_MODEL // (2 * GLA_HEADS)
GLA_DV = D_MODEL // GLA_HEADS
GLA_QK_W = GLA_HEADS * GLA_DK
GLA_V_W = GLA_HEADS * GLA_DV
GLA_LOWRANK = 16
GLA_TAU = 16.0
GLA_CHUNK = 64
HEAD_DIM = 128
N_KV_HEADS = D_MODEL // (2 * HEAD_DIM)
DIL_GROUPS = ((128, 1), (512, 4), (2048, 16))
N_DIL_GROUPS = 3
MAX_WINDOW = 2048
BAND = 128
ROPE_THETA = 10000.0
N_EXPERT_GROUPS = 4
EXPERTS_PER_GROUP = 8
N_EXPERTS = N_EXPERT_GROUPS * EXPERTS_PER_GROUP
TOP_K_INNER = 2
D_EXPERT = D_MODEL // 4
PLE_DIM = 256
NORM_EPS = 1e-6

kernel_name = 'yoco_gla_dilated_hmoe_step'


def rmsnorm(x, g):
    xf = x.astype(jnp.float32)
    y = xf * lax.rsqrt(jnp.mean(xf * xf, axis=-1, keepdims=True) + NORM_EPS)
    return (y * g.astype(jnp.float32)).astype(x.dtype)


def rope(x, pos):
    half = HEAD_DIM // 2
    inv = ROPE_THETA ** (-jnp.arange(half, dtype=jnp.float32) / half)
    ang = pos.astype(jnp.float32)[:, None] * inv[None, :]
    cos = jnp.cos(ang)[:, None, :]
    sin = jnp.sin(ang)[:, None, :]
    xf = x.astype(jnp.float32)
    x1, x2 = xf[..., :half], xf[..., half:]
    return jnp.concatenate([x1 * cos - x2 * sin, x2 * cos + x1 * sin], axis=-1).astype(x.dtype)


def gla_chunk(S, q, k, v, g):
    c = q.shape[1]
    b = jnp.cumsum(g, axis=1)
    qe = q * jnp.exp(b)
    ke = k * jnp.exp(-b)
    o = jnp.einsum('bchk,bhkv->bchv', qe, S)
    a = jnp.einsum('bihk,bjhk->bhij', qe, ke)
    a = jnp.where(jnp.tril(jnp.ones((c, c), dtype=bool)), a, 0.0)
    o = o + jnp.einsum('bhij,bjhv->bihv', a, v)
    b_last = b[:, -1]
    kd = k * jnp.exp(b_last[:, None] - b)
    S_new = jnp.exp(b_last)[..., None] * S + jnp.einsum('bchk,bchv->bhkv', kd, v)
    return S_new, o


def gla_mixer(u, S0, w_in, w_a2, b_a, g_out, w_out):
    B_, T, _ = u.shape
    f32 = jnp.float32
    z = u @ w_in
    q, k, v, r, a1 = jnp.split(z, [GLA_QK_W, 2 * GLA_QK_W, 2 * GLA_QK_W + GLA_V_W,
                                   2 * GLA_QK_W + 2 * GLA_V_W], axis=-1)
    g = jax.nn.log_sigmoid((a1 @ w_a2 + b_a).astype(f32)) / GLA_TAU
    q = q.astype(f32).reshape(B_, T, GLA_HEADS, GLA_DK) * (GLA_DK ** -0.5)
    k = k.astype(f32).reshape(B_, T, GLA_HEADS, GLA_DK)
    v = v.astype(f32).reshape(B_, T, GLA_HEADS, GLA_DV)
    g = g.reshape(B_, T, GLA_HEADS, GLA_DK)
    c = math.gcd(T, GLA_CHUNK)
    n = T // c

    def chunks(a):
        return a.reshape(B_, n, c, *a.shape[2:]).swapaxes(0, 1)

    S, o = lax.scan(lambda s, xs: gla_chunk(s, *xs), S0.astype(f32),
                    (chunks(q), chunks(k), chunks(v), chunks(g)))
    o = o.swapaxes(0, 1).reshape(B_, T, GLA_HEADS, GLA_DV)
    o = rmsnorm(o, g_out).reshape(B_, T, GLA_V_W).astype(u.dtype) * jax.nn.silu(r)
    return o @ w_out, S


def softmax_stats(s, valid):
    s = jnp.where(valid, s.astype(jnp.float32), -jnp.inf)
    m = jnp.max(s, axis=-1, keepdims=True)
    e = jnp.exp(s - m)
    l = jnp.sum(e, axis=-1, keepdims=True)
    return e / l, (m + jnp.log(l))[..., 0]


def dilated_band_attention(q, k, v, window, dil):
    B_, S, H, E = q.shape
    span = window // dil
    L = S // dil
    nb = -(-L // BAND)
    Lp = nb * BAND

    def by_residue(a):
        a = a.reshape(B_, L, dil, H, E).transpose(0, 2, 1, 3, 4)
        return jnp.pad(a, ((0, 0), (0, 0), (0, Lp - L), (0, 0), (0, 0)))

    def with_prev(a):
        a = jnp.pad(a, ((0, 0), (0, 0), (BAND, 0), (0, 0), (0, 0))).reshape(B_, dil, nb + 1, BAND, H, E)
        return jnp.concatenate([a[:, :, :-1], a[:, :, 1:]], axis=3)

    qb = by_residue(q).reshape(B_, dil, nb, BAND, H, E)
    kb = with_prev(by_residue(k))
    vb = with_prev(by_residue(v))
    s = jnp.einsum('brnqhe,brnkhe->brnhqk', qb, kb)
    iq = jnp.arange(BAND)[:, None]
    jk = jnp.arange(2 * BAND)[None, :]
    dist = BAND + iq - jk
    blk = jnp.arange(nb)[:, None, None]
    valid = (dist >= 0) & (dist <= span) & ((blk - 1) * BAND + jk >= 0)
    p, lse = softmax_stats(s, valid[:, None])
    o = jnp.einsum('brnhqk,brnkhe->brnqhe', p, vb.astype(jnp.float32))
    o = o.reshape(B_, dil, Lp, H, E)[:, :, :L].transpose(0, 2, 1, 3, 4).reshape(B_, S, H, E)
    lse = lse.transpose(0, 1, 2, 4, 3).reshape(B_, dil, Lp, H)[:, :, :L]
    lse = lse.transpose(0, 2, 1, 3).reshape(B_, S, H)
    return o, lse


def dilated_gather_attention(q, k_all, v_all, window, dil):
    J = q.shape[1]
    W = k_all.shape[1] - J
    span = window // dil
    steps = dil * jnp.arange(span + 1)
    jq = jnp.arange(J)[:, None]
    idx = W + jq - steps[None, :]
    pos = PAST_LEN + jq - steps[None, :]
    valid = (idx >= 0) & (pos >= 0)
    idx = jnp.maximum(idx, 0)
    kg = k_all[:, idx]
    vg = v_all[:, idx]
    s = jnp.einsum('bjhe,bjkhe->bjhk', q, kg)
    p, lse = softmax_stats(s, valid[:, None, :])
    o = jnp.einsum('bjhk,bjkhe->bjhe', p, vg.astype(jnp.float32))
    return o, lse


def merge_groups(outs, lses):
    wts = jax.nn.softmax(jnp.stack(lses), axis=0)
    return jnp.einsum('gbth,gbthe->bthe', wts, jnp.stack(outs))


def hier_moe(u, w_rg, w_re, w_gu, w_dn):
    shp = u.shape
    f32 = jnp.float32
    x = u.reshape(-1, D_MODEL)
    N = x.shape[0]
    pg = jax.nn.softmax((x @ w_rg).astype(f32), axis=-1)
    pg_top, g_top = lax.top_k(pg, 1)
    le = (x @ w_re).astype(f32).reshape(N, N_EXPERT_GROUPS, EXPERTS_PER_GROUP)
    le_g = jnp.take_along_axis(le, g_top[:, :, None], axis=1)[:, 0]
    lt, it = lax.top_k(le_g, TOP_K_INNER)
    wt = jax.nn.softmax(lt, axis=-1) * pg_top
    gates = jnp.zeros((N, N_EXPERTS), f32).at[jnp.arange(N)[:, None], g_top * EXPERTS_PER_GROUP + it].add(wt)
    gates = gates.reshape(N, N_EXPERT_GROUPS, EXPERTS_PER_GROUP)
    y = jnp.zeros((N, D_MODEL), f32)
    for gi in range(N_EXPERT_GROUPS):
        sl = slice(gi * EXPERTS_PER_GROUP, (gi + 1) * EXPERTS_PER_GROUP)
        hg = jnp.einsum('nd,edf->nef', x, w_gu[sl])
        a, b = jnp.split(hg, 2, axis=-1)
        hact = jax.nn.silu(a) * b * gates[:, gi, :, None].astype(hg.dtype)
        y = y + jnp.einsum('nef,efd->nd', hact, w_dn[sl])
    return y.astype(u.dtype).reshape(shp)


def ffn_and_ple(h, p_i, w, i):
    h = h + hier_moe(rmsnorm(h, w['norm_ffn'][i]), w['moe_w_rg'][i], w['moe_w_re'][i],
                     w['moe_w_gu'][i], w['moe_w_dn'][i])
    gate = jax.nn.sigmoid(rmsnorm(h, w['norm_ple'][i]) @ w['ple_w_gate'][i])
    return h + gate * (p_i @ w['ple_w_proj'][i])


def run_group(x, p, pos, gla_state, k_past, v_past, w):
    n_a = DEPTH // 2
    B_, T, _ = x.shape
    h = x
    new_states = []
    for i in range(n_a):
        if gla_state is None:
            S0 = jnp.zeros((B_, GLA_HEADS, GLA_DK, GLA_DV), jnp.float32)
        else:
            S0 = gla_state[i]
        o, S = gla_mixer(rmsnorm(h, w['norm_mix'][i]), S0, w['gla_w_in'][i], w['gla_w_a2'][i],
                         w['gla_b_a'][i], w['gla_g_out'][i], w['gla_w_out'][i])
        h = h + o
        h = ffn_and_ple(h, p[i], w, i)
        new_states.append(S)
    kv = (rmsnorm(h, w['norm_kv']) @ w['w_kv']).reshape(B_, T, 2, N_KV_HEADS, HEAD_DIM)
    k = rope(kv[:, :, 0], pos)
    v = kv[:, :, 1]
    if k_past is not None:
        k_all = jnp.concatenate([k_past.astype(k.dtype), k], axis=1)
        v_all = jnp.concatenate([v_past.astype(v.dtype), v], axis=1)
    for j in range(DEPTH - n_a):
        li = n_a + j
        u = rmsnorm(h, w['norm_mix'][li])
        q = rope((u @ w['dil_w_q'][j]).reshape(B_, T, N_DIL_GROUPS * N_KV_HEADS, HEAD_DIM), pos)
        q = (q * (HEAD_DIM ** -0.5)).reshape(B_, T, N_DIL_GROUPS, N_KV_HEADS, HEAD_DIM)
        outs, lses = [], []
        for gi, (win, dil) in enumerate(DIL_GROUPS):
            if k_past is None:
                o_g, l_g = dilated_band_attention(q[:, :, gi], k, v, win, dil)
            else:
                o_g, l_g = dilated_gather_attention(q[:, :, gi], k_all, v_all, win, dil)
            outs.append(o_g)
            lses.append(l_g)
        o = merge_groups(outs, lses).reshape(B_, T, N_KV_HEADS * HEAD_DIM).astype(x.dtype)
        h = h + o @ w['dil_w_out'][j]
        h = ffn_and_ple(h, p[li], w, li)
    y = rmsnorm(h, w['norm_final'])
    if k_past is None:
        rows = min(MAX_WINDOW, T)
        k_new, v_new = k[:, T - rows:], v[:, T - rows:]
    else:
        k_new, v_new = k, v
    return y, jnp.stack(new_states).astype(x.dtype), k_new, v_new


def setup_inputs(seed: int = 0) -> dict:
    key = jax.random.key(seed)
    keys = iter(jax.random.split(key, 32))
    f32 = jnp.float32
    n_a = DEPTH // 2
    n_b = DEPTH - n_a
    w_buf = min(MAX_WINDOW, PAST_LEN)

    def normal(shape, scale=1.0):
        return jax.random.normal(next(keys), shape, f32) * scale

    def gain(shape):
        return 1.0 + 0.05 * normal(shape)

    gla_in_w = 2 * GLA_QK_W + 2 * GLA_V_W + GLA_LOWRANK
    return {
        'x_prompt': normal((BATCH, SEQ, D_MODEL)),
        'x_sample': normal((DEC_BATCH, DEC_SEQ, D_MODEL)),
        'p_prompt': normal((DEPTH, BATCH, SEQ, PLE_DIM)),
        'p_sample': normal((DEPTH, DEC_BATCH, DEC_SEQ, PLE_DIM)),
        'state_gla': normal((n_a, DEC_BATCH, GLA_HEADS, GLA_DK, GLA_DV)),
        'cache_k': normal((DEC_BATCH, w_buf, N_KV_HEADS, HEAD_DIM)),
        'cache_v': normal((DEC_BATCH, w_buf, N_KV_HEADS, HEAD_DIM)),
        'norm_mix': gain((DEPTH, D_MODEL)),
        'norm_ffn': gain((DEPTH, D_MODEL)),
        'norm_ple': gain((DEPTH, D_MODEL)),
        'norm_kv': gain((D_MODEL,)),
        'norm_final': gain((D_MODEL,)),
        'gla_w_in': normal((n_a, D_MODEL, gla_in_w), D_MODEL ** -0.5),
        'gla_w_a2': normal((n_a, GLA_LOWRANK, GLA_QK_W), GLA_LOWRANK ** -0.5),
        'gla_b_a': normal((n_a, GLA_QK_W), 0.1),
        'gla_g_out': gain((n_a, GLA_DV)),
        'gla_w_out': normal((n_a, GLA_V_W, D_MODEL), GLA_V_W ** -0.5),
        'w_kv': normal((D_MODEL, 2 * N_KV_HEADS * HEAD_DIM), D_MODEL ** -0.5),
        'dil_w_q': normal((n_b, D_MODEL, N_DIL_GROUPS * N_KV_HEADS * HEAD_DIM), D_MODEL ** -0.5),
        'dil_w_out': normal((n_b, N_KV_HEADS * HEAD_DIM, D_MODEL), (N_KV_HEADS * HEAD_DIM) ** -0.5),
        'moe_w_rg': normal((DEPTH, D_MODEL, N_EXPERT_GROUPS), D_MODEL ** -0.5),
        'moe_w_re': normal((DEPTH, D_MODEL, N_EXPERTS), D_MODEL ** -0.5),
        'moe_w_gu': normal((DEPTH, N_EXPERTS, D_MODEL, 2 * D_EXPERT), D_MODEL ** -0.5),
        'moe_w_dn': normal((DEPTH, N_EXPERTS, D_EXPERT, D_MODEL), D_EXPERT ** -0.5),
        'ple_w_gate': normal((DEPTH, D_MODEL, D_MODEL), D_MODEL ** -0.5),
        'ple_w_proj': normal((DEPTH, PLE_DIM, D_MODEL), PLE_DIM ** -0.5),
    }


def reference(x_prompt, x_sample, p_prompt, p_sample, state_gla, cache_k, cache_v,
              norm_mix, norm_ffn, norm_ple, norm_kv, norm_final,
              gla_w_in, gla_w_a2, gla_b_a, gla_g_out, gla_w_out,
              w_kv, dil_w_q, dil_w_out,
              moe_w_rg, moe_w_re, moe_w_gu, moe_w_dn,
              ple_w_gate, ple_w_proj):
    w = dict(norm_mix=norm_mix, norm_ffn=norm_ffn, norm_ple=norm_ple, norm_kv=norm_kv,
             norm_final=norm_final, gla_w_in=gla_w_in, gla_w_a2=gla_w_a2, gla_b_a=gla_b_a,
             gla_g_out=gla_g_out, gla_w_out=gla_w_out, w_kv=w_kv, dil_w_q=dil_w_q,
             dil_w_out=dil_w_out, moe_w_rg=moe_w_rg, moe_w_re=moe_w_re, moe_w_gu=moe_w_gu,
             moe_w_dn=moe_w_dn, ple_w_gate=ple_w_gate, ple_w_proj=ple_w_proj)
    pos_p = jnp.arange(x_prompt.shape[1])
    pos_s = PAST_LEN + jnp.arange(x_sample.shape[1])
    y_p, sg_p, k_p, v_p = run_group(x_prompt, p_prompt, pos_p, None, None, None, w)
    y_s, sg_s, k_s, v_s = run_group(x_sample, p_sample, pos_s, state_gla, cache_k, cache_v, w)
    return (y_p, y_s, sg_p, sg_s, k_p, v_p, k_s, v_s)
```

```python
import functools

import numpy as np
import jax
import jax.numpy as jnp
from jax import lax
from jax.experimental import pallas as pl
from jax.experimental.pallas import tpu as pltpu

F32 = jnp.float32
BF16 = jnp.bfloat16
HIGHEST = lax.Precision.HIGHEST

NORM_EPS = 1e-6
GLA_TAU = 16.0
GLA_CHUNK = 64
GLA_PAD = 128
HEAD_DIM = 128
BAND = 128
DIL_GROUPS = ((128, 1), (512, 4), (2048, 16))
ROPE_THETA = 10000.0
N_EXPERT_GROUPS = 4
EXPERTS_PER_GROUP = 8
N_EXPERTS = N_EXPERT_GROUPS * EXPERTS_PER_GROUP
NEG_BIG = -1e30
MIB = 1024 * 1024


def _params(semantics, vmem_mib):
    return pltpu.CompilerParams(dimension_semantics=semantics, vmem_limit_bytes=vmem_mib * MIB)


def _rms(x, gain):
    var = jnp.mean(x * x, axis=-1, keepdims=True)
    return x * lax.rsqrt(var + NORM_EPS) * gain


def _sigmoid(x):
    return 1.0 / (1.0 + jnp.exp(-x))


def _norm_linear_kernel(x_ref, g_ref, w_ref, *rest, rope, scale):
    if rope:
        cos_ref, sin_ref, o_ref, xn_ref = rest
    else:
        o_ref, xn_ref = rest

    @pl.when(pl.program_id(1) == 0)
    def _():
        xn_ref[...] = _rms(x_ref[...], g_ref[...]).astype(BF16)

    acc = jnp.dot(xn_ref[...], w_ref[...], preferred_element_type=F32)
    if rope:
        cos = cos_ref[...]
        sin = sin_ref[...]
        parts = []
        for c in range(acc.shape[1] // HEAD_DIM):
            y = acc[:, c * HEAD_DIM:(c + 1) * HEAD_DIM]
            parts.append(y * cos + pltpu.roll(y, HEAD_DIM // 2, 1) * sin)
        acc = parts[0] if len(parts) == 1 else jnp.concatenate(parts, axis=1)
    if scale != 1.0:
        acc = acc * scale
    o_ref[...] = acc


def _norm_linear(x, gain, w, *, col_start, n_cols, tn, tm, rope=None, scale=1.0):
    n, k = x.shape
    grid = (n // tm, n_cols // tn)
    c0 = col_start // tn
    in_specs = [
        pl.BlockSpec((tm, k), lambda i, j: (i, 0)),
        pl.BlockSpec((1, k), lambda i, j: (0, 0)),
        pl.BlockSpec((k, tn), lambda i, j: (0, c0 + j)),
    ]
    args = [x, gain, w]
    if rope is not None:
        in_specs += [pl.BlockSpec((tm, HEAD_DIM), lambda i, j: (i, 0))] * 2
        args += list(rope)
    return pl.pallas_call(
        functools.partial(_norm_linear_kernel, rope=rope is not None, scale=scale),
        out_shape=jax.ShapeDtypeStruct((n, n_cols), F32),
        grid=grid,
        in_specs=in_specs,
        out_specs=pl.BlockSpec((tm, tn), lambda i, j: (i, j)),
        scratch_shapes=[pltpu.VMEM((tm, k), BF16)],
        compiler_params=_params(("parallel", "arbitrary"), 40),
        name="norm_linear",
    )(*args)


def _gla_gate_kernel(x_ref, gn_ref, w1_ref, w2_ref, b_ref, o_ref):
    xn = _rms(x_ref[...], gn_ref[...]).astype(BF16)
    a1 = jnp.dot(xn, w1_ref[...], preferred_element_type=F32)
    pre = jnp.dot(a1, w2_ref[...], precision=HIGHEST, preferred_element_type=F32) + b_ref[...]
    o_ref[...] = (jnp.minimum(pre, 0.0) - jnp.log1p(jnp.exp(-jnp.abs(pre)))) * (1.0 / GLA_TAU)


def _gla_gate(x, gain, w1, w2, b_a, *, tm):
    n, k = x.shape
    r, qk = w2.shape
    return pl.pallas_call(
        _gla_gate_kernel,
        out_shape=jax.ShapeDtypeStruct((n, qk), F32),
        grid=(n // tm,),
        in_specs=[
            pl.BlockSpec((tm, k), lambda i: (i, 0)),
            pl.BlockSpec((1, k), lambda i: (0, 0)),
            pl.BlockSpec((k, r), lambda i: (0, 0)),
            pl.BlockSpec((r, qk), lambda i: (0, 0)),
            pl.BlockSpec((1, qk), lambda i: (0, 0)),
        ],
        out_specs=pl.BlockSpec((tm, qk), lambda i: (i, 0)),
        compiler_params=_params(("parallel",), 32),
        name="gla_gate",
    )(x, gain, w1, w2, b_a)


def _gla_kernel(*refs, c, has_s0, q_scale):
    if has_s0:
        q_ref, k_ref, v_ref, g_ref, s0_ref, o_ref, so_ref, s_scr = refs
    else:
        q_ref, k_ref, v_ref, g_ref, o_ref, so_ref, s_scr = refs
    n = pl.program_id(2)

    @pl.when(n == 0)
    def _():
        if has_s0:
            s_scr[...] = s0_ref[...]
        else:
            s_scr[...] = jnp.zeros_like(s_scr)

    def pad(a):
        if c == GLA_PAD:
            return a
        return jnp.concatenate([a, jnp.zeros((GLA_PAD - c, a.shape[1]), a.dtype)], axis=0)

    row = lax.broadcasted_iota(jnp.int32, (GLA_PAD, GLA_PAD), 0)
    col = lax.broadcasted_iota(jnp.int32, (GLA_PAD, GLA_PAD), 1)
    tri = jnp.where(row >= col, 1.0, 0.0).astype(F32)
    bp = jnp.dot(tri, pad(g_ref[...]), precision=HIGHEST, preferred_element_type=F32)
    b = bp[:c]
    k = k_ref[...]
    v = v_ref[...]
    qe = (q_ref[...] * q_scale * jnp.exp(b)).astype(BF16)
    ke = (k * jnp.exp(-b)).astype(BF16)
    s_old = s_scr[...]
    o = jnp.dot(qe, s_old.astype(BF16), preferred_element_type=F32)
    a = lax.dot_general(qe, ke, (((1,), (1,)), ((), ())), preferred_element_type=F32)
    rc = lax.broadcasted_iota(jnp.int32, (c, c), 0)
    cc = lax.broadcasted_iota(jnp.int32, (c, c), 1)
    a = jnp.where(rc >= cc, a, 0.0)
    o = o + jnp.dot(a.astype(BF16), v.astype(BF16), preferred_element_type=F32)
    o_ref[...] = o

    k_t = pad(k).T
    b_t = bp.T
    b_last = b_t[:, c - 1:c]
    kd_t = (k_t * jnp.exp(b_last - b_t)).astype(BF16)
    s_new = jnp.exp(b_last) * s_old + jnp.dot(kd_t, pad(v).astype(BF16), preferred_element_type=F32)
    s_scr[...] = s_new

    @pl.when(n == pl.num_programs(2) - 1)
    def _():
        so_ref[...] = s_new


def _gla_scan(z, g, s0, *, row0, batch, seq, heads, dk, dv, c):
    nchunk = seq // c
    rb0 = row0 // c
    kb = heads
    vb = 2 * heads * dk // dv

    def rows(b, h, n):
        return rb0 + b * nchunk + n

    in_specs = [
        pl.BlockSpec((c, dk), lambda b, h, n: (rows(b, h, n), h)),
        pl.BlockSpec((c, dk), lambda b, h, n: (rows(b, h, n), kb + h)),
        pl.BlockSpec((c, dv), lambda b, h, n: (rows(b, h, n), vb + h)),
        pl.BlockSpec((c, dk), lambda b, h, n: (rows(b, h, n), h)),
    ]
    args = [z, z, z, g]
    if s0 is not None:
        in_specs.append(pl.BlockSpec((None, None, dk, dv), lambda b, h, n: (b, h, 0, 0)))
        args.append(s0)
    return pl.pallas_call(
        functools.partial(_gla_kernel, c=c, has_s0=s0 is not None, q_scale=float(dk) ** -0.5),
        out_shape=(jax.ShapeDtypeStruct((batch * seq, heads * dv), F32),
                   jax.ShapeDtypeStruct((batch, heads, dk, dv), F32)),
        grid=(batch, heads, nchunk),
        in_specs=in_specs,
        out_specs=(pl.BlockSpec((c, dv), lambda b, h, n: (b * nchunk + n, h)),
                   pl.BlockSpec((None, None, dk, dv), lambda b, h, n: (b, h, 0, 0))),
        scratch_shapes=[pltpu.VMEM((dk, dv), F32)],
        compiler_params=_params(("parallel", "parallel", "arbitrary"), 32),
        name="gla_scan",
    )(*args)


def _gla_out_kernel(o_ref, r_ref, x_ref, go_ref, w_ref, h_ref, *, heads):
    o = o_ref[...]
    dv = o.shape[1] // heads
    go = go_ref[...]
    parts = [_rms(o[:, h * dv:(h + 1) * dv], go) for h in range(heads)]
    on = jnp.concatenate(parts, axis=1)
    r = r_ref[...]
    y = (on * (r * _sigmoid(r))).astype(BF16)
    h_ref[...] = x_ref[...] + jnp.dot(y, w_ref[...], preferred_element_type=F32)


def _gla_out(o, z, x, g_out, w_out, *, heads, tm):
    n, d = x.shape
    vw = o.shape[1]
    rblk = (z.shape[1] - vw) // vw
    return pl.pallas_call(
        functools.partial(_gla_out_kernel, heads=heads),
        out_shape=jax.ShapeDtypeStruct((n, d), F32),
        grid=(n // tm,),
        in_specs=[
            pl.BlockSpec((tm, vw), lambda i: (i, 0)),
            pl.BlockSpec((tm, vw), lambda i: (i, rblk)),
            pl.BlockSpec((tm, d), lambda i: (i, 0)),
            pl.BlockSpec((1, vw // heads), lambda i: (0, 0)),
            pl.BlockSpec((vw, d), lambda i: (0, 0)),
        ],
        out_specs=pl.BlockSpec((tm, d), lambda i: (i, 0)),
        compiler_params=_params(("parallel",), 44),
        name="gla_out",
    )(o, z, x, g_out, w_out)


def _router_kernel(x_ref, g_ref, w_ref, xn_ref, info_ref):
    xn = _rms(x_ref[...], g_ref[...])
    xn_ref[...] = xn.astype(BF16)
    lg = jnp.dot(xn, w_ref[...], precision=HIGHEST, preferred_element_type=F32)
    lane = lax.broadcasted_iota(jnp.int32, lg.shape, 1).astype(F32)
    far = float(lg.shape[1])
    is_grp = lane < N_EXPERT_GROUPS
    lgm = jnp.where(is_grp, lg, NEG_BIG)
    gmax = jnp.max(lgm, axis=-1, keepdims=True)
    gsum = jnp.sum(jnp.where(is_grp, jnp.exp(lg - gmax), 0.0), axis=-1, keepdims=True)
    p_top = 1.0 / gsum
    g_top = jnp.min(jnp.where(lgm == gmax, lane, far), axis=-1, keepdims=True)
    lo = N_EXPERT_GROUPS + g_top * EXPERTS_PER_GROUP
    in_grp = (lane >= lo) & (lane < lo + EXPERTS_PER_GROUP)
    le = jnp.where(in_grp, lg, NEG_BIG)
    m0 = jnp.max(le, axis=-1, keepdims=True)
    i0 = jnp.min(jnp.where(le == m0, lane, far), axis=-1, keepdims=True)
    le1 = jnp.where(lane == i0, NEG_BIG, le)
    m1 = jnp.max(le1, axis=-1, keepdims=True)
    i1 = jnp.min(jnp.where(le1 == m1, lane, far), axis=-1, keepdims=True)
    t = jnp.exp(m1 - m0)
    w0 = p_top / (1.0 + t)
    w1 = p_top * t / (1.0 + t)
    info = jnp.where(lane == 0.0, i0 - N_EXPERT_GROUPS,
                     jnp.where(lane == 1.0, i1 - N_EXPERT_GROUPS,
                               jnp.where(lane == 2.0, w0, jnp.where(lane == 3.0, w1, 0.0))))
    info_ref[...] = info


def _router(x, gain, w_router, *, tm):
    n, k = x.shape
    lanes = w_router.shape[1]
    return pl.pallas_call(
        _router_kernel,
        out_shape=(jax.ShapeDtypeStruct((n, k), BF16), jax.ShapeDtypeStruct((n, lanes), F32)),
        grid=(n // tm,),
        in_specs=[
            pl.BlockSpec((tm, k), lambda i: (i, 0)),
            pl.BlockSpec((1, k), lambda i: (0, 0)),
            pl.BlockSpec((k, lanes), lambda i: (0, 0)),
        ],
        out_specs=(pl.BlockSpec((tm, k), lambda i: (i, 0)), pl.BlockSpec((tm, lanes), lambda i: (i, 0))),
        compiler_params=_params(("parallel",), 32),
        name="moe_router",
    )(x, gain, w_router)


def _experts_kernel(te_ref, nu_ref, x_ref, gate_ref, wgu_ref, wdn_ref, o_ref, wgu_bf, wdn_bf):
    t = pl.program_id(0)

    @pl.when(t < nu_ref[0])
    def _():
        prev = te_ref[jnp.maximum(t - 1, 0)]

        @pl.when((t == 0) | (te_ref[t] != prev))
        def _():
            wgu_bf[...] = wgu_ref[...].astype(BF16)
            wdn_bf[...] = wdn_ref[...].astype(BF16)

        hg = jnp.dot(x_ref[...], wgu_bf[...], preferred_element_type=F32)
        f = hg.shape[1] // 2
        a = hg[:, :f]
        hact = (a * _sigmoid(a)) * hg[:, f:] * gate_ref[...]
        o_ref[...] = jnp.dot(hact.astype(BF16), wdn_bf[...], preferred_element_type=F32)


def _experts(x_sorted, gate_sorted, tile_expert, n_used, w_gu, w_dn, *, layer, tm):
    p, d = x_sorted.shape
    f2 = w_gu.shape[-1]
    f = w_dn.shape[-2]
    grid_spec = pltpu.PrefetchScalarGridSpec(
        num_scalar_prefetch=2,
        grid=(p // tm,),
        in_specs=[
            pl.BlockSpec((tm, d), lambda t, te, nu: (t, 0)),
            pl.BlockSpec((tm, 1), lambda t, te, nu: (t, 0)),
            pl.BlockSpec((None, None, d, f2), lambda t, te, nu: (layer, te[t], 0, 0)),
            pl.BlockSpec((None, None, f, d), lambda t, te, nu: (layer, te[t], 0, 0)),
        ],
        out_specs=pl.BlockSpec((tm, d), lambda t, te, nu: (t, 0)),
        scratch_shapes=[pltpu.VMEM((d, f2), BF16), pltpu.VMEM((f, d), BF16)],
    )
    return pl.pallas_call(
        _experts_kernel,
        out_shape=jax.ShapeDtypeStruct((p, d), F32),
        grid_spec=grid_spec,
        compiler_params=_params(("arbitrary",), 52),
        name="moe_experts",
    )(tile_expert, n_used, x_sorted, gate_sorted, w_gu, w_dn)


def _route(info, *, tm):
    n = info.shape[0]
    e = info[:, :2].astype(jnp.int32).reshape(-1)
    w = info[:, 2:4].reshape(-1)
    onehot = (e[:, None] == jnp.arange(N_EXPERTS, dtype=jnp.int32)[None, :]).astype(jnp.int32)
    before = jnp.cumsum(onehot, axis=0) - onehot
    counts = jnp.sum(onehot, axis=0)
    padded = ((counts + tm - 1) // tm) * tm
    ends = jnp.cumsum(padded)
    starts = ends - padded
    pos = jnp.sum(onehot * (starts[None, :] + before), axis=1)
    p = ((2 * n + N_EXPERTS * (tm - 1)) // tm + 1) * tm
    row_token = jnp.zeros((p,), jnp.int32).at[pos].set(jnp.arange(2 * n, dtype=jnp.int32) // 2)
    row_gate = jnp.zeros((p,), F32).at[pos].set(w)
    tile_start = jnp.arange(p // tm, dtype=jnp.int32) * tm
    n_used = (ends[-1] // tm).astype(jnp.int32)
    te = jnp.sum((tile_start[:, None] >= ends[None, :]).astype(jnp.int32), axis=1)
    last = jnp.sum((jnp.maximum(ends[-1] - 1, 0) >= ends).astype(jnp.int32))
    te = jnp.where(tile_start < ends[-1], te, last).astype(jnp.int32)
    pos2 = pos.reshape(n, 2)
    return row_token, row_gate[:, None], te, n_used.reshape(1), pos2[:, 0], pos2[:, 1]


def _ple_kernel(h_ref, y0_ref, y1_ref, p_ref, gp_ref, wg_ref, wp_ref, *rest, final):
    if final:
        gf_ref, o_ref = rest
    else:
        (o_ref,) = rest
    h = h_ref[...] + (y0_ref[...] + y1_ref[...])
    hn = _rms(h, gp_ref[...]).astype(BF16)
    gate = _sigmoid(jnp.dot(hn, wg_ref[...], preferred_element_type=F32))
    proj = jnp.dot(p_ref[...].astype(BF16), wp_ref[...], preferred_element_type=F32)
    out = h + gate * proj
    if final:
        out = _rms(out, gf_ref[...])
    o_ref[...] = out


def _ple(h, y0, y1, p_emb, g_ple, w_gate, w_proj, g_final, *, tm):
    n, d = h.shape
    pd = p_emb.shape[1]
    row = lambda i: (i, 0)
    fix = lambda i: (0, 0)
    in_specs = [
        pl.BlockSpec((tm, d), row), pl.BlockSpec((tm, d), row), pl.BlockSpec((tm, d), row),
        pl.BlockSpec((tm, pd), row), pl.BlockSpec((1, d), fix),
        pl.BlockSpec((d, d), fix), pl.BlockSpec((pd, d), fix),
    ]
    args = [h, y0, y1, p_emb, g_ple, w_gate, w_proj]
    if g_final is not None:
        in_specs.append(pl.BlockSpec((1, d), fix))
        args.append(g_final)
    return pl.pallas_call(
        functools.partial(_ple_kernel, final=g_final is not None),
        out_shape=jax.ShapeDtypeStruct((n, d), F32),
        grid=(n // tm,),
        in_specs=in_specs,
        out_specs=pl.BlockSpec((tm, d), row),
        compiler_params=_params(("parallel",), 48),
        name="ple",
    )(*args)


def _linear_res_kernel(a_ref, w_ref, h_ref, o_ref):
    o_ref[...] = h_ref[...] + jnp.dot(a_ref[...].astype(BF16), w_ref[...], preferred_element_type=F32)


def _linear_res(a, w, h, *, tm):
    n, d = h.shape
    k = a.shape[1]
    return pl.pallas_call(
        _linear_res_kernel,
        out_shape=jax.ShapeDtypeStruct((n, d), F32),
        grid=(n // tm,),
        in_specs=[pl.BlockSpec((tm, k), lambda i: (i, 0)), pl.BlockSpec((k, d), lambda i: (0, 0)),
                  pl.BlockSpec((tm, d), lambda i: (i, 0))],
        out_specs=pl.BlockSpec((tm, d), lambda i: (i, 0)),
        compiler_params=_params(("parallel",), 40),
        name="attn_out",
    )(a, w, h)


def _band_attn_kernel(q0_ref, q1_ref, q2_ref, k_ref, v_ref, o_ref, og_ref, lse_ref, *, seq):
    q_refs = (q0_ref, q1_ref, q2_ref)
    row = lax.broadcasted_iota(jnp.int32, (BAND, BAND), 0)
    col = lax.broadcasted_iota(jnp.int32, (BAND, BAND), 1)
    cur_ok = col <= row
    prev_ok = col >= row
    nt = (((1,), (1,)), ((), ()))

    def rows(start, dil):
        return pl.ds(start, BAND) if dil == 1 else pl.ds(start, BAND, stride=dil)

    for gi, (win, dil) in enumerate(DIL_GROUPS):
        assert win // dil == BAND
        sub_len = seq // dil
        for r in range(dil):
            kp = vp = None
            for n in range(sub_len // BAND):
                sl = rows(r + dil * BAND * n, dil)
                qb = q_refs[gi][sl, :].astype(BF16)
                kc = k_ref[sl, :].astype(BF16)
                vc = v_ref[sl, :].astype(BF16)
                s_c = jnp.where(cur_ok, lax.dot_general(qb, kc, nt, preferred_element_type=F32), NEG_BIG)
                m = jnp.max(s_c, axis=-1, keepdims=True)
                if kp is not None:
                    s_p = jnp.where(prev_ok, lax.dot_general(qb, kp, nt, preferred_element_type=F32), NEG_BIG)
                    m = jnp.maximum(m, jnp.max(s_p, axis=-1, keepdims=True))
                p_c = jnp.exp(s_c - m)
                l = jnp.sum(p_c, axis=-1, keepdims=True)
                acc = jnp.dot(p_c.astype(BF16), vc, preferred_element_type=F32)
                if kp is not None:
                    p_p = jnp.exp(s_p - m)
                    l = l + jnp.sum(p_p, axis=-1, keepdims=True)
                    acc = acc + jnp.dot(p_p.astype(BF16), vp, preferred_element_type=F32)
                og_ref[gi, sl, :] = acc / l
                lse_ref[gi, sl, :] = jnp.broadcast_to(m + jnp.log(l), (BAND, HEAD_DIM))
                kp, vp = kc, vc

    l0, l1, l2 = lse_ref[0], lse_ref[1], lse_ref[2]
    mx = jnp.maximum(jnp.maximum(l0, l1), l2)
    w0, w1, w2 = jnp.exp(l0 - mx), jnp.exp(l1 - mx), jnp.exp(l2 - mx)
    o_ref[...] = (w0 * og_ref[0] + w1 * og_ref[1] + w2 * og_ref[2]) / (w0 + w1 + w2)


def _band_attention(q, k, v, *, batch, seq, n_heads):
    blk = (seq, HEAD_DIM)
    qspec = lambda gi: pl.BlockSpec(blk, lambda b, h: (b, gi * n_heads + h))
    kv = pl.BlockSpec(blk, lambda b, h: (b, h))
    return pl.pallas_call(
        functools.partial(_band_attn_kernel, seq=seq),
        out_shape=jax.ShapeDtypeStruct((batch * seq, n_heads * HEAD_DIM), F32),
        grid=(batch, n_heads),
        in_specs=[qspec(0), qspec(1), qspec(2), kv, kv],
        out_specs=kv,
        scratch_shapes=[pltpu.VMEM((3, seq, HEAD_DIM), F32), pltpu.VMEM((3, seq, HEAD_DIM), F32)],
        compiler_params=_params(("parallel", "parallel"), 40),
        name="band_attention",
    )(q, q, q, k, v)


def _decode_rows(w_buf, dec_seq):
    max_dil = max(d for _, d in DIL_GROUPS)
    dense_from = w_buf
    for win, dil in DIL_GROUPS:
        if dil < max_dil:
            dense_from = min(dense_from, w_buf - win)
    dense_from = max((dense_from // max_dil) * max_dil, 0)
    return max_dil, dense_from


def _decode_bias(w_buf, dec_seq, key_pos, n_pad):
    bias = np.full((len(DIL_GROUPS) * dec_seq, n_pad), NEG_BIG, np.float32)
    for gi, (win, dil) in enumerate(DIL_GROUPS):
        for j in range(dec_seq):
            dist = (w_buf + j) - key_pos
            ok = (dist >= 0) & (dist <= win) & (dist % dil == 0)
            bias[gi * dec_seq + j, :len(key_pos)][ok] = 0.0
    return bias


def _decode_attn_kernel(q_ref, kn_ref, vn_ref, ka_ref, kb_ref, va_ref, vb_ref, bias_ref, o_ref, *,
                        n_heads, dec_seq, n_groups, n_pad):
    bias = bias_ref[...]
    nt = (((1,), (1,)), ((), ()))
    na = ka_ref.shape[0] * (ka_ref.shape[1] // n_heads)
    nb = kb_ref.shape[0] * (kb_ref.shape[1] // n_heads)
    tail = n_pad - na - nb - dec_seq

    def head_rows(a_ref, b_ref, n_ref, h):
        ra = a_ref.shape[1] // n_heads
        rb = b_ref.shape[1] // n_heads
        xa = a_ref[:, pl.ds(h, ra, stride=n_heads), :].reshape(na, HEAD_DIM)
        xb = b_ref[:, pl.ds(h, rb, stride=n_heads), :].reshape(nb, HEAD_DIM)
        xn = n_ref[:, h * HEAD_DIM:(h + 1) * HEAD_DIM]
        parts = [xa, xb, xn]
        if tail:
            parts.append(jnp.zeros((tail, HEAD_DIM), F32))
        return jnp.concatenate(parts, axis=0).astype(BF16)

    outs = []
    for h in range(n_heads):
        kh = head_rows(ka_ref, kb_ref, kn_ref, h)
        vh = head_rows(va_ref, vb_ref, vn_ref, h)
        qh = jnp.concatenate(
            [q_ref[:, (gi * n_heads + h) * HEAD_DIM:(gi * n_heads + h + 1) * HEAD_DIM] for gi in range(n_groups)],
            axis=0).astype(BF16)
        s = lax.dot_general(qh, kh, nt, preferred_element_type=F32) + bias
        m = jnp.max(s, axis=-1, keepdims=True)
        p = jnp.exp(s - m)
        l = jnp.sum(p, axis=-1, keepdims=True)
        og = jnp.dot(p.astype(BF16), vh, preferred_element_type=F32) / l
        lse = m + jnp.log(l)
        ls = [lse[gi * dec_seq:(gi + 1) * dec_seq] for gi in range(n_groups)]
        mx = functools.reduce(jnp.maximum, ls)
        ws = [jnp.exp(x - mx) for x in ls]
        num = sum(w * og[gi * dec_seq:(gi + 1) * dec_seq] for gi, w in enumerate(ws))
        outs.append(num / sum(ws))
    o_ref[...] = jnp.concatenate(outs, axis=1)


def _decode_attention(q, k_new, v_new, cache_k, cache_v, *, row0, n_heads):
    batch, w_buf = cache_k.shape[:2]
    dec_seq = (q.shape[0] - row0) // batch
    n_groups = len(DIL_GROUPS)
    comb, dense_from = _decode_rows(w_buf, dec_seq)
    assert dec_seq <= comb and comb % dec_seq == 0 and dec_seq % 8 == 0 and w_buf % comb == 0
    n_comb = dense_from // comb
    n_dense = (w_buf - dense_from) // comb
    key_pos = np.concatenate([
        (np.arange(n_comb)[:, None] * comb + np.arange(dec_seq)[None, :]).reshape(-1),
        dense_from + np.arange(w_buf - dense_from),
        w_buf + np.arange(dec_seq)])
    n_pad = -(-len(key_pos) // 128) * 128
    bias = jnp.asarray(_decode_bias(w_buf, dec_seq, key_pos, n_pad))
    ck = cache_k.reshape(batch, w_buf // comb, comb * n_heads, HEAD_DIM)
    cv = cache_v.reshape(batch, w_buf // comb, comb * n_heads, HEAD_DIM)
    rb0 = row0 // dec_seq
    tok = lambda width: pl.BlockSpec((dec_seq, width), lambda b: (rb0 + b, 0))
    comb_spec = pl.BlockSpec((None, n_comb, dec_seq * n_heads, HEAD_DIM), lambda b: (b, 0, 0, 0))
    dense_spec = pl.BlockSpec((None, n_dense, comb * n_heads, HEAD_DIM), lambda b: (b, n_comb // n_dense, 0, 0))
    assert n_comb % n_dense == 0
    return pl.pallas_call(
        functools.partial(_decode_attn_kernel, n_heads=n_heads, dec_seq=dec_seq, n_groups=n_groups, n_pad=n_pad),
        out_shape=jax.ShapeDtypeStruct((batch * dec_seq, n_heads * HEAD_DIM), F32),
        grid=(batch,),
        in_specs=[tok(q.shape[1]), tok(k_new.shape[1]), tok(v_new.shape[1]),
                  comb_spec, dense_spec, comb_spec, dense_spec,
                  pl.BlockSpec(bias.shape, lambda b: (0, 0))],
        out_specs=pl.BlockSpec((dec_seq, n_heads * HEAD_DIM), lambda b: (b, 0)),
        compiler_params=_params(("parallel",), 48),
        name="decode_attention",
    )(q, k_new, v_new, ck, ck, cv, cv, bias)


def _moe_and_ple(h, p_emb, layer, w, g_final, *, tm_moe):
    n, d = h.shape
    lanes = 128
    w_router = jnp.zeros((d, lanes), F32)
    w_router = w_router.at[:, :N_EXPERT_GROUPS].set(w["moe_w_rg"][layer])
    w_router = w_router.at[:, N_EXPERT_GROUPS:N_EXPERT_GROUPS + N_EXPERTS].set(w["moe_w_re"][layer])
    xn, info = _router(h, w["norm_ffn"][layer][None, :], w_router, tm=512)
    row_token, row_gate, te, n_used, pos0, pos1 = _route(info, tm=tm_moe)
    x_sorted = jnp.take(xn, row_token, axis=0)
    y_sorted = _experts(x_sorted, row_gate, te, n_used, w["moe_w_gu"], w["moe_w_dn"], layer=layer, tm=tm_moe)
    y0 = jnp.take(y_sorted, pos0, axis=0)
    y1 = jnp.take(y_sorted, pos1, axis=0)
    return _ple(h, y0, y1, p_emb, w["norm_ple"][layer][None, :], w["ple_w_gate"][layer].astype(BF16),
                w["ple_w_proj"][layer].astype(BF16), g_final, tm=256)


def kernel(x_prompt, x_sample, p_prompt, p_sample, state_gla, cache_k, cache_v, norm_mix, norm_ffn, norm_ple,
           norm_kv, norm_final, gla_w_in, gla_w_a2, gla_b_a, gla_g_out, gla_w_out, w_kv, dil_w_q, dil_w_out,
           moe_w_rg, moe_w_re, moe_w_gu, moe_w_dn, ple_w_gate, ple_w_proj):
    w = dict(norm_ffn=norm_ffn, norm_ple=norm_ple, moe_w_rg=moe_w_rg, moe_w_re=moe_w_re, moe_w_gu=moe_w_gu,
             moe_w_dn=moe_w_dn, ple_w_gate=ple_w_gate, ple_w_proj=ple_w_proj)
    bp, tp, d = x_prompt.shape
    bs, ts, _ = x_sample.shape
    n_p, n_s = bp * tp, bs * ts
    depth = p_prompt.shape[0]
    assert depth == 2 and state_gla.shape[0] == 1
    heads, dk, dv = state_gla.shape[2:]
    n_kv = cache_k.shape[2]
    past_len = cache_k.shape[1]
    qk_w, v_w = heads * dk, heads * dv
    lowrank = gla_w_a2.shape[1]

    x = jnp.concatenate([x_prompt.reshape(n_p, d), x_sample.reshape(n_s, d)], axis=0)
    p_emb = jnp.concatenate([p_prompt.reshape(depth, n_p, -1), p_sample.reshape(depth, n_s, -1)], axis=1)

    half = HEAD_DIM // 2
    inv = ROPE_THETA ** (-jnp.arange(half, dtype=F32) / half)
    pos = jnp.concatenate([jnp.tile(jnp.arange(tp), bp), jnp.tile(past_len + jnp.arange(ts), bs)]).astype(F32)
    ang = pos[:, None] * inv[None, :]
    rope = (jnp.concatenate([jnp.cos(ang), jnp.cos(ang)], axis=1),
            jnp.concatenate([-jnp.sin(ang), jnp.sin(ang)], axis=1))

    g_mix0 = norm_mix[0][None, :]
    w_in = gla_w_in[0].astype(BF16)
    main_w = 2 * qk_w + 2 * v_w
    z = _norm_linear(x, g_mix0, w_in, col_start=0, n_cols=main_w, tn=1024, tm=512)
    w_a1 = jnp.zeros((d, 128), BF16).at[:, :lowrank].set(w_in[:, main_w:])
    w_a2 = jnp.zeros((128, qk_w), F32).at[:lowrank].set(gla_w_a2[0])
    g = _gla_gate(x, g_mix0, w_a1, w_a2, gla_b_a[0][None, :], tm=512)
    c_p = int(np.gcd(tp, GLA_CHUNK))
    c_s = int(np.gcd(ts, GLA_CHUNK))
    o_p, sg_p = _gla_scan(z, g, None, row0=0, batch=bp, seq=tp, heads=heads, dk=dk, dv=dv, c=c_p)
    o_s, sg_s = _gla_scan(z, g, state_gla[0], row0=n_p, batch=bs, seq=ts, heads=heads, dk=dk, dv=dv, c=c_s)
    o = jnp.concatenate([o_p, o_s], axis=0)
    h = _gla_out(o, z, x, gla_g_out[0][None, :], gla_w_out[0].astype(BF16), heads=heads, tm=256)
    h = _moe_and_ple(h, p_emb[0], 0, w, None, tm_moe=256)

    w_kv_b = w_kv.astype(BF16)
    kv_w = n_kv * HEAD_DIM
    g_kv = norm_kv[None, :]
    k_all = _norm_linear(h, g_kv, w_kv_b, col_start=0, n_cols=kv_w, tn=kv_w, tm=512, rope=rope)
    v_all = _norm_linear(h, g_kv, w_kv_b, col_start=kv_w, n_cols=kv_w, tn=kv_w, tm=512)
    q_all = _norm_linear(h, norm_mix[1][None, :], dil_w_q[0].astype(BF16), col_start=0,
                         n_cols=dil_w_q.shape[2], tn=kv_w, tm=512, rope=rope, scale=HEAD_DIM ** -0.5)

    a_p = _band_attention(q_all, k_all, v_all, batch=bp, seq=tp, n_heads=n_kv)
    a_s = _decode_attention(q_all, k_all, v_all, cache_k, cache_v, row0=n_p, n_heads=n_kv)
    attn = jnp.concatenate([a_p, a_s], axis=0)
    h = _linear_res(attn, dil_w_out[0].astype(BF16), h, tm=512)
    y = _moe_and_ple(h, p_emb[1], 1, w, norm_final[None, :], tm_moe=256)

    y_p = y[:n_p].reshape(bp, tp, d)
    y_s = y[n_p:].reshape(bs, ts, d)
    rows = min(past_len, tp)
    k_p = k_all[:n_p].reshape(bp, tp, n_kv, HEAD_DIM)[:, tp - rows:]
    v_p = v_all[:n_p].reshape(bp, tp, n_kv, HEAD_DIM)[:, tp - rows:]
    k_s = k_all[n_p:].reshape(bs, ts, n_kv, HEAD_DIM)
    v_s = v_all[n_p:].reshape(bs, ts, n_kv, HEAD_DIM)
    return (y_p, y_s, sg_p[None], sg_s[None], k_p, v_p, k_s, v_s)
```

```python
import functools

import numpy as np
import jax
import jax.numpy as jnp
from jax import lax
from jax.experimental import pallas as pl
from jax.experimental.pallas import tpu as pltpu

F32 = jnp.float32
BF16 = jnp.bfloat16
HIGHEST = lax.Precision.HIGHEST

NORM_EPS = 1e-6
GLA_TAU = 16.0
GLA_CHUNK = 64
GLA_PAD = 128
HEAD_DIM = 128
BAND = 128
DIL_GROUPS = ((128, 1), (512, 4), (2048, 16))
ROPE_THETA = 10000.0
N_EXPERT_GROUPS = 4
EXPERTS_PER_GROUP = 8
N_EXPERTS = N_EXPERT_GROUPS * EXPERTS_PER_GROUP
NEG_BIG = -1e30
MIB = 1024 * 1024


def _params(semantics, vmem_mib):
    return pltpu.CompilerParams(dimension_semantics=semantics, vmem_limit_bytes=vmem_mib * MIB)


def _rms(x, gain):
    var = jnp.mean(x * x, axis=-1, keepdims=True)
    return x * lax.rsqrt(var + NORM_EPS) * gain


def _sigmoid(x):
    return 1.0 / (1.0 + jnp.exp(-x))


def _norm_linear_kernel(x_ref, g_ref, w_ref, *rest, rope, scale):
    if rope:
        cos_ref, sin_ref, o_ref, xn_ref = rest
    else:
        o_ref, xn_ref = rest

    @pl.when(pl.program_id(1) == 0)
    def _():
        xn_ref[...] = _rms(x_ref[...], g_ref[...]).astype(BF16)

    acc = jnp.dot(xn_ref[...], w_ref[...], preferred_element_type=F32)
    if rope:
        cos = cos_ref[...]
        sin = sin_ref[...]
        parts = []
        for c in range(acc.shape[1] // HEAD_DIM):
            y = acc[:, c * HEAD_DIM:(c + 1) * HEAD_DIM]
            parts.append(y * cos + pltpu.roll(y, HEAD_DIM // 2, 1) * sin)
        acc = parts[0] if len(parts) == 1 else jnp.concatenate(parts, axis=1)
    if scale != 1.0:
        acc = acc * scale
    o_ref[...] = acc


def _norm_linear(x, gain, w, *, col_start, n_cols, tn, tm, rope=None, scale=1.0):
    n, k = x.shape
    grid = (n // tm, n_cols // tn)
    c0 = col_start // tn
    in_specs = [
        pl.BlockSpec((tm, k), lambda i, j: (i, 0)),
        pl.BlockSpec((1, k), lambda i, j: (0, 0)),
        pl.BlockSpec((k, tn), lambda i, j: (0, c0 + j)),
    ]
    args = [x, gain, w]
    if rope is not None:
        in_specs += [pl.BlockSpec((tm, HEAD_DIM), lambda i, j: (i, 0))] * 2
        args += list(rope)
    return pl.pallas_call(
        functools.partial(_norm_linear_kernel, rope=rope is not None, scale=scale),
        out_shape=jax.ShapeDtypeStruct((n, n_cols), F32),
        grid=grid,
        in_specs=in_specs,
        out_specs=pl.BlockSpec((tm, tn), lambda i, j: (i, j)),
        scratch_shapes=[pltpu.VMEM((tm, k), BF16)],
        compiler_params=_params(("parallel", "arbitrary"), 40),
        name="norm_linear",
    )(*args)


def _gla_gate_kernel(x_ref, gn_ref, w1_ref, w2_ref, b_ref, o_ref):
    xn = _rms(x_ref[...], gn_ref[...]).astype(BF16)
    a1 = jnp.dot(xn, w1_ref[...], preferred_element_type=F32)
    pre = jnp.dot(a1, w2_ref[...], precision=HIGHEST, preferred_element_type=F32) + b_ref[...]
    o_ref[...] = (jnp.minimum(pre, 0.0) - jnp.log1p(jnp.exp(-jnp.abs(pre)))) * (1.0 / GLA_TAU)


def _gla_gate(x, gain, w1, w2, b_a, *, tm):
    n, k = x.shape
    r, qk = w2.shape
    return pl.pallas_call(
        _gla_gate_kernel,
        out_shape=jax.ShapeDtypeStruct((n, qk), F32),
        grid=(n // tm,),
        in_specs=[
            pl.BlockSpec((tm, k), lambda i: (i, 0)),
            pl.BlockSpec((1, k), lambda i: (0, 0)),
            pl.BlockSpec((k, r), lambda i: (0, 0)),
            pl.BlockSpec((r, qk), lambda i: (0, 0)),
            pl.BlockSpec((1, qk), lambda i: (0, 0)),
        ],
        out_specs=pl.BlockSpec((tm, qk), lambda i: (i, 0)),
        compiler_params=_params(("parallel",), 32),
        name="gla_gate",
    )(x, gain, w1, w2, b_a)


def _gla_kernel(*refs, c, heads, has_s0, q_scale):
    q_ref, k_ref, v_ref, g_ref = refs[:4]
    s0_ref = refs[4] if has_s0 else None
    o_ref, so_ref, s_scr = refs[-3:]
    n = pl.program_id(1)
    dk = q_ref.shape[1] // heads
    dv = v_ref.shape[1] // heads

    @pl.when(n == 0)
    def _():
        if has_s0:
            s_scr[...] = s0_ref[...]
        else:
            s_scr[...] = jnp.zeros_like(s_scr)

    def pad(a):
        if c == GLA_PAD:
            return a
        return jnp.concatenate([a, jnp.zeros((GLA_PAD - c, a.shape[1]), a.dtype)], axis=0)

    row = lax.broadcasted_iota(jnp.int32, (GLA_PAD, GLA_PAD), 0)
    col = lax.broadcasted_iota(jnp.int32, (GLA_PAD, GLA_PAD), 1)
    tri = jnp.where(row >= col, 1.0, 0.0).astype(F32)
    bp = jnp.dot(tri, pad(g_ref[...]), precision=HIGHEST, preferred_element_type=F32)
    b = bp[:c]
    k = k_ref[...]
    v = v_ref[...]
    qe = (q_ref[...] * q_scale * jnp.exp(b)).astype(BF16)
    ke = (k * jnp.exp(-b)).astype(BF16)
    vb = v.astype(BF16)
    k_t = pad(k).T
    b_t = bp.T
    b_last = b_t[:, c - 1:c]
    kd_t = (k_t * jnp.exp(b_last - b_t)).astype(BF16)
    decay = jnp.exp(b_last)
    vp = pad(v).astype(BF16)
    rc = lax.broadcasted_iota(jnp.int32, (c, c), 0)
    cc = lax.broadcasted_iota(jnp.int32, (c, c), 1)
    last = n == pl.num_programs(1) - 1
    for h in range(heads):
        ks = slice(h * dk, (h + 1) * dk)
        vs = slice(h * dv, (h + 1) * dv)
        s_old = s_scr[h]
        o = jnp.dot(qe[:, ks], s_old.astype(BF16), preferred_element_type=F32)
        a = lax.dot_general(qe[:, ks], ke[:, ks], (((1,), (1,)), ((), ())), preferred_element_type=F32)
        a = jnp.where(rc >= cc, a, 0.0)
        o_ref[:, vs] = o + jnp.dot(a.astype(BF16), vb[:, vs], preferred_element_type=F32)
        s_new = decay[ks] * s_old + jnp.dot(kd_t[ks], vp[:, vs], preferred_element_type=F32)
        s_scr[h] = s_new

        @pl.when(last)
        def _():
            so_ref[h] = s_new


def _gla_scan(z, g, s0, *, row0, batch, seq, heads, dk, dv, c):
    nchunk = seq // c
    rb0 = row0 // c
    qk_w, v_w = heads * dk, heads * dv
    rows = lambda b, n: rb0 + b * nchunk + n
    in_specs = [
        pl.BlockSpec((c, qk_w), lambda b, n: (rows(b, n), 0)),
        pl.BlockSpec((c, qk_w), lambda b, n: (rows(b, n), 1)),
        pl.BlockSpec((c, v_w), lambda b, n: (rows(b, n), 2 * qk_w // v_w)),
        pl.BlockSpec((c, qk_w), lambda b, n: (rows(b, n), 0)),
    ]
    args = [z, z, z, g]
    if s0 is not None:
        in_specs.append(pl.BlockSpec((None, heads, dk, dv), lambda b, n: (b, 0, 0, 0)))
        args.append(s0)
    return pl.pallas_call(
        functools.partial(_gla_kernel, c=c, heads=heads, has_s0=s0 is not None, q_scale=float(dk) ** -0.5),
        out_shape=(jax.ShapeDtypeStruct((batch * seq, v_w), F32),
                   jax.ShapeDtypeStruct((batch, heads, dk, dv), F32)),
        grid=(batch, nchunk),
        in_specs=in_specs,
        out_specs=(pl.BlockSpec((c, v_w), lambda b, n: (b * nchunk + n, 0)),
                   pl.BlockSpec((None, heads, dk, dv), lambda b, n: (b, 0, 0, 0))),
        scratch_shapes=[pltpu.VMEM((heads, dk, dv), F32)],
        compiler_params=_params(("parallel", "arbitrary"), 40),
        name="gla_scan",
    )(*args)


def _two_group_specs(tm, width, n_first):
    t_first = n_first // tm
    return (pl.BlockSpec((tm, width), lambda i: (jnp.minimum(i, t_first - 1), 0)),
            pl.BlockSpec((tm, width), lambda i: (jnp.maximum(i - t_first, 0), 0)))


def _pick_group(first_ref, second_ref, n_first):
    in_first = pl.program_id(0) < n_first // first_ref.shape[0]
    return jnp.where(in_first, first_ref[...], second_ref[...])


def _gla_out_kernel(op_ref, os_ref, r_ref, x_ref, go_ref, w_ref, h_ref, *, heads, n_first):
    o = _pick_group(op_ref, os_ref, n_first)
    dv = o.shape[1] // heads
    go = go_ref[...]
    parts = [_rms(o[:, h * dv:(h + 1) * dv], go) for h in range(heads)]
    on = jnp.concatenate(parts, axis=1)
    r = r_ref[...]
    y = (on * (r * _sigmoid(r))).astype(BF16)
    h_ref[...] = x_ref[...] + jnp.dot(y, w_ref[...], preferred_element_type=F32)


def _gla_out(o_p, o_s, z, x, g_out, w_out, *, heads, tm):
    n, d = x.shape
    vw = o_p.shape[1]
    n_first = o_p.shape[0]
    rblk = (z.shape[1] - vw) // vw
    return pl.pallas_call(
        functools.partial(_gla_out_kernel, heads=heads, n_first=n_first),
        out_shape=jax.ShapeDtypeStruct((n, d), F32),
        grid=(n // tm,),
        in_specs=[
            *_two_group_specs(tm, vw, n_first),
            pl.BlockSpec((tm, vw), lambda i: (i, rblk)),
            pl.BlockSpec((tm, d), lambda i: (i, 0)),
            pl.BlockSpec((1, vw // heads), lambda i: (0, 0)),
            pl.BlockSpec((vw, d), lambda i: (0, 0)),
        ],
        out_specs=pl.BlockSpec((tm, d), lambda i: (i, 0)),
        compiler_params=_params(("parallel",), 44),
        name="gla_out",
    )(o_p, o_s, z, x, g_out, w_out)


def _router_kernel(x_ref, g_ref, w_ref, xn_ref, info_ref):
    xn = _rms(x_ref[...], g_ref[...])
    xn_ref[...] = xn
    lg = jnp.dot(xn, w_ref[...], precision=HIGHEST, preferred_element_type=F32)
    lane = lax.broadcasted_iota(jnp.int32, lg.shape, 1).astype(F32)
    far = float(lg.shape[1])
    is_grp = lane < N_EXPERT_GROUPS
    lgm = jnp.where(is_grp, lg, NEG_BIG)
    gmax = jnp.max(lgm, axis=-1, keepdims=True)
    gsum = jnp.sum(jnp.where(is_grp, jnp.exp(lg - gmax), 0.0), axis=-1, keepdims=True)
    p_top = 1.0 / gsum
    g_top = jnp.min(jnp.where(lgm == gmax, lane, far), axis=-1, keepdims=True)
    lo = N_EXPERT_GROUPS + g_top * EXPERTS_PER_GROUP
    in_grp = (lane >= lo) & (lane < lo + EXPERTS_PER_GROUP)
    le = jnp.where(in_grp, lg, NEG_BIG)
    m0 = jnp.max(le, axis=-1, keepdims=True)
    i0 = jnp.min(jnp.where(le == m0, lane, far), axis=-1, keepdims=True)
    le1 = jnp.where(lane == i0, NEG_BIG, le)
    m1 = jnp.max(le1, axis=-1, keepdims=True)
    i1 = jnp.min(jnp.where(le1 == m1, lane, far), axis=-1, keepdims=True)
    t = jnp.exp(m1 - m0)
    w0 = p_top / (1.0 + t)
    w1 = p_top * t / (1.0 + t)
    info = jnp.where(lane == 0.0, i0 - N_EXPERT_GROUPS,
                     jnp.where(lane == 1.0, i1 - N_EXPERT_GROUPS,
                               jnp.where(lane == 2.0, w0, jnp.where(lane == 3.0, w1, 0.0))))
    info_ref[...] = info


def _router(x, gain, w_router, *, tm):
    n, k = x.shape
    lanes = w_router.shape[1]
    return pl.pallas_call(
        _router_kernel,
        out_shape=(jax.ShapeDtypeStruct((n, k), F32), jax.ShapeDtypeStruct((n, lanes), F32)),
        grid=(n // tm,),
        in_specs=[
            pl.BlockSpec((tm, k), lambda i: (i, 0)),
            pl.BlockSpec((1, k), lambda i: (0, 0)),
            pl.BlockSpec((k, lanes), lambda i: (0, 0)),
        ],
        out_specs=(pl.BlockSpec((tm, k), lambda i: (i, 0)), pl.BlockSpec((tm, lanes), lambda i: (i, 0))),
        compiler_params=_params(("parallel",), 32),
        name="moe_router",
    )(x, gain, w_router)


def _row_gather_copy(src_hbm, src_row, dst, dst_row, sem):
    return pltpu.make_async_copy(src_hbm.at[pl.ds(src_row, 1)], dst.at[pl.ds(dst_row, 1)], sem)


def _start_row_gather(src_hbm, idx_ref, base, dst, sem):
    def body(r, carry):
        _row_gather_copy(src_hbm, idx_ref[base + r], dst, r, sem).start()
        return carry
    lax.fori_loop(0, dst.shape[0], body, 0, unroll=8)


def _wait_row_gather(src_hbm, dst, sem):
    def body(r, carry):
        _row_gather_copy(src_hbm, 0, dst, r, sem).wait()
        return carry
    lax.fori_loop(0, dst.shape[0], body, 0, unroll=8)


def _experts_kernel(te_ref, nu_ref, rt_ref, x_hbm, wgu_ref, wdn_ref, o_ref, xbuf, sem, wgu_bf, wdn_bf):
    t = pl.program_id(0)
    tm = xbuf.shape[1]
    n_used = nu_ref[0]
    slot = lax.rem(t, 2)

    @pl.when((t == 0) & (n_used > 0))
    def _():
        _start_row_gather(x_hbm, rt_ref, 0, xbuf.at[0], sem.at[0])

    @pl.when(t + 1 < n_used)
    def _():
        _start_row_gather(x_hbm, rt_ref, (t + 1) * tm, xbuf.at[1 - slot], sem.at[1 - slot])

    @pl.when(t < n_used)
    def _():
        prev = te_ref[jnp.maximum(t - 1, 0)]

        @pl.when((t == 0) | (te_ref[t] != prev))
        def _():
            wgu_bf[...] = wgu_ref[...].astype(BF16)
            wdn_bf[...] = wdn_ref[...].astype(BF16)

        _wait_row_gather(x_hbm, xbuf.at[slot], sem.at[slot])
        hg = jnp.dot(xbuf[slot].astype(BF16), wgu_bf[...], preferred_element_type=F32)
        f = hg.shape[1] // 2
        a = hg[:, :f]
        hact = (a * _sigmoid(a)) * hg[:, f:]
        o_ref[...] = jnp.dot(hact.astype(BF16), wdn_bf[...], preferred_element_type=F32)

    @pl.when(t >= n_used)
    def _():
        o_ref[...] = jnp.zeros_like(o_ref)


def _experts(xn, row_token, tile_expert, n_used, w_gu, w_dn, *, layer, tm):
    d = xn.shape[1]
    p = row_token.shape[0]
    f2 = w_gu.shape[-1]
    f = w_dn.shape[-2]
    grid_spec = pltpu.PrefetchScalarGridSpec(
        num_scalar_prefetch=3,
        grid=(p // tm,),
        in_specs=[
            pl.BlockSpec(memory_space=pl.ANY),
            pl.BlockSpec((None, None, d, f2), lambda t, te, nu, rt: (layer, te[t], 0, 0)),
            pl.BlockSpec((None, None, f, d), lambda t, te, nu, rt: (layer, te[t], 0, 0)),
        ],
        out_specs=pl.BlockSpec((tm, d), lambda t, te, nu, rt: (t, 0)),
        scratch_shapes=[pltpu.VMEM((2, tm, d), F32), pltpu.SemaphoreType.DMA((2,)),
                        pltpu.VMEM((d, f2), BF16), pltpu.VMEM((f, d), BF16)],
    )
    return pl.pallas_call(
        _experts_kernel,
        out_shape=jax.ShapeDtypeStruct((p, d), F32),
        grid_spec=grid_spec,
        compiler_params=_params(("arbitrary",), 52),
        name="moe_experts",
    )(tile_expert, n_used, row_token, xn, w_gu, w_dn)


def _route(info, *, tm):
    n = info.shape[0]
    e = info[:, :2].astype(jnp.int32).reshape(-1)
    onehot = (e[:, None] == jnp.arange(N_EXPERTS, dtype=jnp.int32)[None, :]).astype(jnp.int32)
    before = jnp.cumsum(onehot, axis=0) - onehot
    counts = jnp.sum(onehot, axis=0)
    padded = ((counts + tm - 1) // tm) * tm
    ends = jnp.cumsum(padded)
    starts = ends - padded
    pos = jnp.sum(onehot * (starts[None, :] + before), axis=1)
    p = ((2 * n + N_EXPERTS * (tm - 1)) // tm + 1) * tm
    row_token = jnp.zeros((p,), jnp.int32).at[pos].set(jnp.arange(2 * n, dtype=jnp.int32) // 2,
                                                       unique_indices=True, indices_are_sorted=False)
    tile_start = jnp.arange(p // tm, dtype=jnp.int32) * tm
    n_used = (ends[-1] // tm).astype(jnp.int32)
    te = jnp.sum((tile_start[:, None] >= ends[None, :]).astype(jnp.int32), axis=1)
    last = jnp.sum((jnp.maximum(ends[-1] - 1, 0) >= ends).astype(jnp.int32))
    te = jnp.where(tile_start < ends[-1], te, last).astype(jnp.int32)
    pos2 = pos.reshape(n, 2)
    return row_token, te, n_used.reshape(1), pos2[:, 0], pos2[:, 1]


def _ple_kernel(pos0_ref, pos1_ref, h_ref, info_ref, y_hbm, p_ref, gp_ref, wg_ref, wp_ref, *rest, final):
    if final:
        gf_ref, o_ref, ybuf, sem = rest
    else:
        o_ref, ybuf, sem = rest
    i = pl.program_id(0)
    tm = h_ref.shape[0]
    slot = lax.rem(i, 2)

    def start(tile, s):
        _start_row_gather(y_hbm, pos0_ref, tile * tm, ybuf.at[s, 0], sem.at[s])
        _start_row_gather(y_hbm, pos1_ref, tile * tm, ybuf.at[s, 1], sem.at[s])

    @pl.when(i == 0)
    def _():
        start(0, 0)

    @pl.when(i + 1 < pl.num_programs(0))
    def _():
        start(i + 1, 1 - slot)

    _wait_row_gather(y_hbm, ybuf.at[slot, 0], sem.at[slot])
    _wait_row_gather(y_hbm, ybuf.at[slot, 1], sem.at[slot])
    info = info_ref[...]
    h = h_ref[...] + (info[:, 2:3] * ybuf[slot, 0] + info[:, 3:4] * ybuf[slot, 1])
    hn = _rms(h, gp_ref[...]).astype(BF16)
    gate = _sigmoid(jnp.dot(hn, wg_ref[...], preferred_element_type=F32))
    proj = jnp.dot(p_ref[...].astype(BF16), wp_ref[...], preferred_element_type=F32)
    out = h + gate * proj
    if final:
        out = _rms(out, gf_ref[...])
    o_ref[...] = out


def _ple(h, info, y_sorted, pos0, pos1, p_emb, g_ple, w_gate, w_proj, g_final, *, tm):
    n, d = h.shape
    pd = p_emb.shape[1]
    row = lambda i, p0, p1: (i, 0)
    fix = lambda i, p0, p1: (0, 0)
    in_specs = [
        pl.BlockSpec((tm, d), row), pl.BlockSpec((tm, info.shape[1]), row), pl.BlockSpec(memory_space=pl.ANY),
        pl.BlockSpec((tm, pd), row), pl.BlockSpec((1, d), fix),
        pl.BlockSpec((d, d), fix), pl.BlockSpec((pd, d), fix),
    ]
    args = [h, info, y_sorted, p_emb, g_ple, w_gate, w_proj]
    if g_final is not None:
        in_specs.append(pl.BlockSpec((1, d), fix))
        args.append(g_final)
    grid_spec = pltpu.PrefetchScalarGridSpec(
        num_scalar_prefetch=2,
        grid=(n // tm,),
        in_specs=in_specs,
        out_specs=pl.BlockSpec((tm, d), row),
        scratch_shapes=[pltpu.VMEM((2, 2, tm, d), F32), pltpu.SemaphoreType.DMA((2,))],
    )
    return pl.pallas_call(
        functools.partial(_ple_kernel, final=g_final is not None),
        out_shape=jax.ShapeDtypeStruct((n, d), F32),
        grid_spec=grid_spec,
        compiler_params=_params(("arbitrary",), 52),
        name="ple",
    )(pos0, pos1, *args)


def _linear_res_kernel(ap_ref, as_ref, w_ref, h_ref, o_ref, *, n_first):
    a = _pick_group(ap_ref, as_ref, n_first).astype(BF16)
    o_ref[...] = h_ref[...] + jnp.dot(a, w_ref[...], preferred_element_type=F32)


def _linear_res(a_p, a_s, w, h, *, tm):
    n, d = h.shape
    k = a_p.shape[1]
    n_first = a_p.shape[0]
    return pl.pallas_call(
        functools.partial(_linear_res_kernel, n_first=n_first),
        out_shape=jax.ShapeDtypeStruct((n, d), F32),
        grid=(n // tm,),
        in_specs=[*_two_group_specs(tm, k, n_first), pl.BlockSpec((k, d), lambda i: (0, 0)),
                  pl.BlockSpec((tm, d), lambda i: (i, 0))],
        out_specs=pl.BlockSpec((tm, d), lambda i: (i, 0)),
        compiler_params=_params(("parallel",), 40),
        name="attn_out",
    )(a_p, a_s, w, h)


def _band_attn_kernel(q0_ref, q1_ref, q2_ref, k_ref, v_ref, o_ref, og_ref, lse_ref, *, seq):
    q_refs = (q0_ref, q1_ref, q2_ref)
    row = lax.broadcasted_iota(jnp.int32, (BAND, BAND), 0)
    col = lax.broadcasted_iota(jnp.int32, (BAND, BAND), 1)
    cur_ok = col <= row
    prev_ok = col >= row
    nt = (((1,), (1,)), ((), ()))

    def rows(start, dil):
        return pl.ds(start, BAND) if dil == 1 else pl.ds(start, BAND, stride=dil)

    for gi, (win, dil) in enumerate(DIL_GROUPS):
        assert win // dil == BAND
        sub_len = seq // dil
        for r in range(dil):
            kp = vp = None
            for n in range(sub_len // BAND):
                sl = rows(r + dil * BAND * n, dil)
                qb = q_refs[gi][sl, :].astype(BF16)
                kc = k_ref[sl, :].astype(BF16)
                vc = v_ref[sl, :].astype(BF16)
                s_c = jnp.where(cur_ok, lax.dot_general(qb, kc, nt, preferred_element_type=F32), NEG_BIG)
                m = jnp.max(s_c, axis=-1, keepdims=True)
                if kp is not None:
                    s_p = jnp.where(prev_ok, lax.dot_general(qb, kp, nt, preferred_element_type=F32), NEG_BIG)
                    m = jnp.maximum(m, jnp.max(s_p, axis=-1, keepdims=True))
                p_c = jnp.exp(s_c - m)
                l = jnp.sum(p_c, axis=-1, keepdims=True)
                acc = jnp.dot(p_c.astype(BF16), vc, preferred_element_type=F32)
                if kp is not None:
                    p_p = jnp.exp(s_p - m)
                    l = l + jnp.sum(p_p, axis=-1, keepdims=True)
                    acc = acc + jnp.dot(p_p.astype(BF16), vp, preferred_element_type=F32)
                og_ref[gi, sl, :] = acc / l
                lse_ref[gi, sl, :] = jnp.broadcast_to(m + jnp.log(l), (BAND, HEAD_DIM))
                kp, vp = kc, vc

    l0, l1, l2 = lse_ref[0], lse_ref[1], lse_ref[2]
    mx = jnp.maximum(jnp.maximum(l0, l1), l2)
    w0, w1, w2 = jnp.exp(l0 - mx), jnp.exp(l1 - mx), jnp.exp(l2 - mx)
    o_ref[...] = (w0 * og_ref[0] + w1 * og_ref[1] + w2 * og_ref[2]) / (w0 + w1 + w2)


def _band_attention(q, k, v, *, batch, seq, n_heads):
    blk = (seq, HEAD_DIM)
    qspec = lambda gi: pl.BlockSpec(blk, lambda b, h: (b, gi * n_heads + h))
    kv = pl.BlockSpec(blk, lambda b, h: (b, h))
    return pl.pallas_call(
        functools.partial(_band_attn_kernel, seq=seq),
        out_shape=jax.ShapeDtypeStruct((batch * seq, n_heads * HEAD_DIM), F32),
        grid=(batch, n_heads),
        in_specs=[qspec(0), qspec(1), qspec(2), kv, kv],
        out_specs=kv,
        scratch_shapes=[pltpu.VMEM((3, seq, HEAD_DIM), F32), pltpu.VMEM((3, seq, HEAD_DIM), F32)],
        compiler_params=_params(("parallel", "parallel"), 40),
        name="band_attention",
    )(q, q, q, k, v)


def _decode_rows(w_buf, dec_seq):
    max_dil = max(d for _, d in DIL_GROUPS)
    dense_from = w_buf
    for win, dil in DIL_GROUPS:
        if dil < max_dil:
            dense_from = min(dense_from, w_buf - win)
    dense_from = max((dense_from // max_dil) * max_dil, 0)
    return max_dil, dense_from


def _decode_bias(w_buf, dec_seq, key_pos, n_pad):
    bias = np.full((len(DIL_GROUPS) * dec_seq, n_pad), NEG_BIG, np.float32)
    for gi, (win, dil) in enumerate(DIL_GROUPS):
        for j in range(dec_seq):
            dist = (w_buf + j) - key_pos
            ok = (dist >= 0) & (dist <= win) & (dist % dil == 0)
            bias[gi * dec_seq + j, :len(key_pos)][ok] = 0.0
    return bias


def _decode_attn_kernel(q_ref, kn_ref, vn_ref, ka_ref, kb_ref, va_ref, vb_ref, bias_ref, o_ref, *,
                        n_heads, dec_seq, n_groups, n_pad):
    bias = bias_ref[...]
    nt = (((1,), (1,)), ((), ()))
    na = ka_ref.shape[0] * (ka_ref.shape[1] // n_heads)
    nb = kb_ref.shape[0] * (kb_ref.shape[1] // n_heads)
    tail = n_pad - na - nb - dec_seq

    def head_rows(a_ref, b_ref, n_ref, h):
        ra = a_ref.shape[1] // n_heads
        rb = b_ref.shape[1] // n_heads
        xa = a_ref[:, pl.ds(h, ra, stride=n_heads), :].reshape(na, HEAD_DIM)
        xb = b_ref[:, pl.ds(h, rb, stride=n_heads), :].reshape(nb, HEAD_DIM)
        xn = n_ref[:, h * HEAD_DIM:(h + 1) * HEAD_DIM]
        parts = [xa, xb, xn]
        if tail:
            parts.append(jnp.zeros((tail, HEAD_DIM), F32))
        return jnp.concatenate(parts, axis=0).astype(BF16)

    outs = []
    for h in range(n_heads):
        kh = head_rows(ka_ref, kb_ref, kn_ref, h)
        vh = head_rows(va_ref, vb_ref, vn_ref, h)
        qh = jnp.concatenate(
            [q_ref[:, (gi * n_heads + h) * HEAD_DIM:(gi * n_heads + h + 1) * HEAD_DIM] for gi in range(n_groups)],
            axis=0).astype(BF16)
        s = lax.dot_general(qh, kh, nt, preferred_element_type=F32) + bias
        m = jnp.max(s, axis=-1, keepdims=True)
        p = jnp.exp(s - m)
        l = jnp.sum(p, axis=-1, keepdims=True)
        og = jnp.dot(p.astype(BF16), vh, preferred_element_type=F32) / l
        lse = m + jnp.log(l)
        ls = [lse[gi * dec_seq:(gi + 1) * dec_seq] for gi in range(n_groups)]
        mx = functools.reduce(jnp.maximum, ls)
        ws = [jnp.exp(x - mx) for x in ls]
        num = sum(w * og[gi * dec_seq:(gi + 1) * dec_seq] for gi, w in enumerate(ws))
        outs.append(num / sum(ws))
    o_ref[...] = jnp.concatenate(outs, axis=1)


def _decode_attention(q, k_new, v_new, cache_k, cache_v, *, row0, n_heads):
    batch, w_buf = cache_k.shape[:2]
    dec_seq = (q.shape[0] - row0) // batch
    n_groups = len(DIL_GROUPS)
    comb, dense_from = _decode_rows(w_buf, dec_seq)
    assert dec_seq <= comb and comb % dec_seq == 0 and dec_seq % 8 == 0 and w_buf % comb == 0
    n_comb = dense_from // comb
    n_dense = (w_buf - dense_from) // comb
    key_pos = np.concatenate([
        (np.arange(n_comb)[:, None] * comb + np.arange(dec_seq)[None, :]).reshape(-1),
        dense_from + np.arange(w_buf - dense_from),
        w_buf + np.arange(dec_seq)])
    n_pad = -(-len(key_pos) // 128) * 128
    bias = jnp.asarray(_decode_bias(w_buf, dec_seq, key_pos, n_pad))
    ck = cache_k.reshape(batch, w_buf // comb, comb * n_heads, HEAD_DIM)
    cv = cache_v.reshape(batch, w_buf // comb, comb * n_heads, HEAD_DIM)
    rb0 = row0 // dec_seq
    tok = lambda width: pl.BlockSpec((dec_seq, width), lambda b: (rb0 + b, 0))
    comb_spec = pl.BlockSpec((None, n_comb, dec_seq * n_heads, HEAD_DIM), lambda b: (b, 0, 0, 0))
    dense_spec = pl.BlockSpec((None, n_dense, comb * n_heads, HEAD_DIM), lambda b: (b, n_comb // n_dense, 0, 0))
    assert n_comb % n_dense == 0
    return pl.pallas_call(
        functools.partial(_decode_attn_kernel, n_heads=n_heads, dec_seq=dec_seq, n_groups=n_groups, n_pad=n_pad),
        out_shape=jax.ShapeDtypeStruct((batch * dec_seq, n_heads * HEAD_DIM), F32),
        grid=(batch,),
        in_specs=[tok(q.shape[1]), tok(k_new.shape[1]), tok(v_new.shape[1]),
                  comb_spec, dense_spec, comb_spec, dense_spec,
                  pl.BlockSpec(bias.shape, lambda b: (0, 0))],
        out_specs=pl.BlockSpec((dec_seq, n_heads * HEAD_DIM), lambda b: (b, 0)),
        compiler_params=_params(("parallel",), 48),
        name="decode_attention",
    )(q, k_new, v_new, ck, ck, cv, cv, bias)


def _moe_and_ple(h, p_emb, layer, w, g_final, *, tm_moe):
    n, d = h.shape
    lanes = 128
    w_router = jnp.zeros((d, lanes), F32)
    w_router = w_router.at[:, :N_EXPERT_GROUPS].set(w["moe_w_rg"][layer])
    w_router = w_router.at[:, N_EXPERT_GROUPS:N_EXPERT_GROUPS + N_EXPERTS].set(w["moe_w_re"][layer])
    xn, info = _router(h, w["norm_ffn"][layer][None, :], w_router, tm=512)
    row_token, te, n_used, pos0, pos1 = _route(info, tm=tm_moe)
    y_sorted = _experts(xn, row_token, te, n_used, w["moe_w_gu"], w["moe_w_dn"], layer=layer, tm=tm_moe)
    return _ple(h, info, y_sorted, pos0, pos1, p_emb, w["norm_ple"][layer][None, :],
                w["ple_w_gate"][layer].astype(BF16), w["ple_w_proj"][layer].astype(BF16), g_final, tm=256)


def kernel(x_prompt, x_sample, p_prompt, p_sample, state_gla, cache_k, cache_v, norm_mix, norm_ffn, norm_ple,
           norm_kv, norm_final, gla_w_in, gla_w_a2, gla_b_a, gla_g_out, gla_w_out, w_kv, dil_w_q, dil_w_out,
           moe_w_rg, moe_w_re, moe_w_gu, moe_w_dn, ple_w_gate, ple_w_proj):
    w = dict(norm_ffn=norm_ffn, norm_ple=norm_ple, moe_w_rg=moe_w_rg, moe_w_re=moe_w_re, moe_w_gu=moe_w_gu,
             moe_w_dn=moe_w_dn, ple_w_gate=ple_w_gate, ple_w_proj=ple_w_proj)
    bp, tp, d = x_prompt.shape
    bs, ts, _ = x_sample.shape
    n_p, n_s = bp * tp, bs * ts
    depth = p_prompt.shape[0]
    assert depth == 2 and state_gla.shape[0] == 1
    heads, dk, dv = state_gla.shape[2:]
    n_kv = cache_k.shape[2]
    past_len = cache_k.shape[1]
    qk_w, v_w = heads * dk, heads * dv
    lowrank = gla_w_a2.shape[1]

    x = jnp.concatenate([x_prompt.reshape(n_p, d), x_sample.reshape(n_s, d)], axis=0)
    p_emb = jnp.concatenate([p_prompt.reshape(depth, n_p, -1), p_sample.reshape(depth, n_s, -1)], axis=1)

    half = HEAD_DIM // 2
    inv = ROPE_THETA ** (-jnp.arange(half, dtype=F32) / half)
    pos = jnp.concatenate([jnp.tile(jnp.arange(tp), bp), jnp.tile(past_len + jnp.arange(ts), bs)]).astype(F32)
    ang = pos[:, None] * inv[None, :]
    rope = (jnp.concatenate([jnp.cos(ang), jnp.cos(ang)], axis=1),
            jnp.concatenate([-jnp.sin(ang), jnp.sin(ang)], axis=1))

    g_mix0 = norm_mix[0][None, :]
    w_in = gla_w_in[0].astype(BF16)
    main_w = 2 * qk_w + 2 * v_w
    z = _norm_linear(x, g_mix0, w_in, col_start=0, n_cols=main_w, tn=1024, tm=512)
    w_a1 = jnp.zeros((d, 128), BF16).at[:, :lowrank].set(w_in[:, main_w:])
    w_a2 = jnp.zeros((128, qk_w), F32).at[:lowrank].set(gla_w_a2[0])
    g = _gla_gate(x, g_mix0, w_a1, w_a2, gla_b_a[0][None, :], tm=512)
    c_p = int(np.gcd(tp, GLA_CHUNK))
    c_s = int(np.gcd(ts, GLA_CHUNK))
    gla = functools.partial(_gla_scan, z, g, heads=heads, dk=dk, dv=dv)
    o_p, sg_p = gla(None, row0=0, batch=bp, seq=tp, c=c_p)
    o_s, sg_s = gla(state_gla[0], row0=n_p, batch=bs, seq=ts, c=c_s)
    h = _gla_out(o_p, o_s, z, x, gla_g_out[0][None, :], gla_w_out[0].astype(BF16), heads=heads, tm=256)
    h = _moe_and_ple(h, p_emb[0], 0, w, None, tm_moe=256)

    w_kv_b = w_kv.astype(BF16)
    kv_w = n_kv * HEAD_DIM
    g_kv = norm_kv[None, :]
    k_all = _norm_linear(h, g_kv, w_kv_b, col_start=0, n_cols=kv_w, tn=kv_w, tm=512, rope=rope)
    v_all = _norm_linear(h, g_kv, w_kv_b, col_start=kv_w, n_cols=kv_w, tn=kv_w, tm=512)
    q_all = _norm_linear(h, norm_mix[1][None, :], dil_w_q[0].astype(BF16), col_start=0,
                         n_cols=dil_w_q.shape[2], tn=kv_w, tm=512, rope=rope, scale=HEAD_DIM ** -0.5)

    a_p = _band_attention(q_all, k_all, v_all, batch=bp, seq=tp, n_heads=n_kv)
    a_s = _decode_attention(q_all, k_all, v_all, cache_k, cache_v, row0=n_p, n_heads=n_kv)
    h = _linear_res(a_p, a_s, dil_w_out[0].astype(BF16), h, tm=512)
    y = _moe_and_ple(h, p_emb[1], 1, w, norm_final[None, :], tm_moe=256)

    y_p = y[:n_p].reshape(bp, tp, d)
    y_s = y[n_p:].reshape(bs, ts, d)
    rows = min(past_len, tp)
    k_p = k_all[:n_p].reshape(bp, tp, n_kv, HEAD_DIM)[:, tp - rows:]
    v_p = v_all[:n_p].reshape(bp, tp, n_kv, HEAD_DIM)[:, tp - rows:]
    k_s = k_all[n_p:].reshape(bs, ts, n_kv, HEAD_DIM)
    v_s = v_all[n_p:].reshape(bs, ts, n_kv, HEAD_DIM)
    return (y_p, y_s, sg_p[None], sg_s[None], k_p, v_p, k_s, v_s)
```

```python
import functools

import numpy as np
import jax
import jax.numpy as jnp
from jax import lax
from jax.experimental import pallas as pl
from jax.experimental.pallas import tpu as pltpu

F32 = jnp.float32
BF16 = jnp.bfloat16
HIGHEST = lax.Precision.HIGHEST

NORM_EPS = 1e-6
GLA_TAU = 16.0
GLA_CHUNK = 64
GLA_PAD = 128
HEAD_DIM = 128
BAND = 128
DIL_GROUPS = ((128, 1), (512, 4), (2048, 16))
ROPE_THETA = 10000.0
N_EXPERT_GROUPS = 4
EXPERTS_PER_GROUP = 8
N_EXPERTS = N_EXPERT_GROUPS * EXPERTS_PER_GROUP
NEG_BIG = -1e30
MIB = 1024 * 1024


def _params(semantics, vmem_mib):
    return pltpu.CompilerParams(dimension_semantics=semantics, vmem_limit_bytes=vmem_mib * MIB)


def _rms(x, gain):
    var = jnp.mean(x * x, axis=-1, keepdims=True)
    return x * lax.rsqrt(var + NORM_EPS) * gain


def _sigmoid(x):
    return 1.0 / (1.0 + jnp.exp(-x))


def _norm_linear_kernel(x_ref, g_ref, w_ref, *rest, rope, scale):
    if rope:
        cos_ref, sin_ref, o_ref, xn_ref = rest
    else:
        o_ref, xn_ref = rest

    @pl.when(pl.program_id(1) == 0)
    def _():
        xn_ref[...] = _rms(x_ref[...], g_ref[...]).astype(BF16)

    acc = jnp.dot(xn_ref[...], w_ref[...], preferred_element_type=F32)
    if rope:
        cos = cos_ref[...]
        sin = sin_ref[...]
        parts = []
        for c in range(acc.shape[1] // HEAD_DIM):
            y = acc[:, c * HEAD_DIM:(c + 1) * HEAD_DIM]
            parts.append(y * cos + pltpu.roll(y, HEAD_DIM // 2, 1) * sin)
        acc = parts[0] if len(parts) == 1 else jnp.concatenate(parts, axis=1)
    if scale != 1.0:
        acc = acc * scale
    o_ref[...] = acc


def _norm_linear(x, gain, w, *, col_start, n_cols, tn, tm, rope=None, scale=1.0, row0=0, n_rows=None):
    k = x.shape[1]
    n = x.shape[0] if n_rows is None else n_rows
    grid = (n // tm, n_cols // tn)
    c0 = col_start // tn
    r0 = row0 // tm
    in_specs = [
        pl.BlockSpec((tm, k), lambda i, j: (r0 + i, 0)),
        pl.BlockSpec((1, k), lambda i, j: (0, 0)),
        pl.BlockSpec((k, tn), lambda i, j: (0, c0 + j)),
    ]
    args = [x, gain, w]
    if rope is not None:
        in_specs += [pl.BlockSpec((tm, HEAD_DIM), lambda i, j: (r0 + i, 0))] * 2
        args += list(rope)
    return pl.pallas_call(
        functools.partial(_norm_linear_kernel, rope=rope is not None, scale=scale),
        out_shape=jax.ShapeDtypeStruct((n, n_cols), F32),
        grid=grid,
        in_specs=in_specs,
        out_specs=pl.BlockSpec((tm, tn), lambda i, j: (i, j)),
        scratch_shapes=[pltpu.VMEM((tm, k), BF16)],
        compiler_params=_params(("parallel", "arbitrary"), 40),
        name="norm_linear",
    )(*args)


def _gla_gate_kernel(x_ref, gn_ref, w1_ref, w2_ref, b_ref, o_ref):
    xn = _rms(x_ref[...], gn_ref[...]).astype(BF16)
    a1 = jnp.dot(xn, w1_ref[...], preferred_element_type=F32)
    pre = jnp.dot(a1, w2_ref[...], precision=HIGHEST, preferred_element_type=F32) + b_ref[...]
    o_ref[...] = (jnp.minimum(pre, 0.0) - jnp.log1p(jnp.exp(-jnp.abs(pre)))) * (1.0 / GLA_TAU)


def _gla_gate(x, gain, w1, w2, b_a, *, tm):
    n, k = x.shape
    r, qk = w2.shape
    return pl.pallas_call(
        _gla_gate_kernel,
        out_shape=jax.ShapeDtypeStruct((n, qk), F32),
        grid=(n // tm,),
        in_specs=[
            pl.BlockSpec((tm, k), lambda i: (i, 0)),
            pl.BlockSpec((1, k), lambda i: (0, 0)),
            pl.BlockSpec((k, r), lambda i: (0, 0)),
            pl.BlockSpec((r, qk), lambda i: (0, 0)),
            pl.BlockSpec((1, qk), lambda i: (0, 0)),
        ],
        out_specs=pl.BlockSpec((tm, qk), lambda i: (i, 0)),
        compiler_params=_params(("parallel",), 32),
        name="gla_gate",
    )(x, gain, w1, w2, b_a)


def _gla_kernel(*refs, c, heads, has_s0, q_scale):
    q_ref, k_ref, v_ref, g_ref = refs[:4]
    s0_ref = refs[4] if has_s0 else None
    o_ref, so_ref, s_scr = refs[-3:]
    n = pl.program_id(1)
    dk = q_ref.shape[1] // heads
    dv = v_ref.shape[1] // heads

    @pl.when(n == 0)
    def _():
        if has_s0:
            s_scr[...] = s0_ref[...]
        else:
            s_scr[...] = jnp.zeros_like(s_scr)

    def pad(a):
        if c == GLA_PAD:
            return a
        return jnp.concatenate([a, jnp.zeros((GLA_PAD - c, a.shape[1]), a.dtype)], axis=0)

    row = lax.broadcasted_iota(jnp.int32, (GLA_PAD, GLA_PAD), 0)
    col = lax.broadcasted_iota(jnp.int32, (GLA_PAD, GLA_PAD), 1)
    tri = jnp.where(row >= col, 1.0, 0.0).astype(F32)
    bp = jnp.dot(tri, pad(g_ref[...]), precision=HIGHEST, preferred_element_type=F32)
    b = bp[:c]
    k = k_ref[...]
    v = v_ref[...]
    qe = (q_ref[...] * q_scale * jnp.exp(b)).astype(BF16)
    ke = (k * jnp.exp(-b)).astype(BF16)
    vb = v.astype(BF16)
    k_t = pad(k).T
    b_t = bp.T
    b_last = b_t[:, c - 1:c]
    kd_t = (k_t * jnp.exp(b_last - b_t)).astype(BF16)
    decay = jnp.exp(b_last)
    vp = pad(v).astype(BF16)
    rc = lax.broadcasted_iota(jnp.int32, (c, c), 0)
    cc = lax.broadcasted_iota(jnp.int32, (c, c), 1)
    last = n == pl.num_programs(1) - 1
    for h in range(heads):
        ks = slice(h * dk, (h + 1) * dk)
        vs = slice(h * dv, (h + 1) * dv)
        s_old = s_scr[h]
        o = jnp.dot(qe[:, ks], s_old.astype(BF16), preferred_element_type=F32)
        a = lax.dot_general(qe[:, ks], ke[:, ks], (((1,), (1,)), ((), ())), preferred_element_type=F32)
        a = jnp.where(rc >= cc, a, 0.0)
        o_ref[:, vs] = o + jnp.dot(a.astype(BF16), vb[:, vs], preferred_element_type=F32)
        s_new = decay[ks] * s_old + jnp.dot(kd_t[ks], vp[:, vs], preferred_element_type=F32)
        s_scr[h] = s_new

        @pl.when(last)
        def _():
            so_ref[h] = s_new


def _gla_scan(z, g, s0, *, row0, batch, seq, heads, dk, dv, c):
    nchunk = seq // c
    rb0 = row0 // c
    qk_w, v_w = heads * dk, heads * dv
    rows = lambda b, n: rb0 + b * nchunk + n
    in_specs = [
        pl.BlockSpec((c, qk_w), lambda b, n: (rows(b, n), 0)),
        pl.BlockSpec((c, qk_w), lambda b, n: (rows(b, n), 1)),
        pl.BlockSpec((c, v_w), lambda b, n: (rows(b, n), 2 * qk_w // v_w)),
        pl.BlockSpec((c, qk_w), lambda b, n: (rows(b, n), 0)),
    ]
    args = [z, z, z, g]
    if s0 is not None:
        in_specs.append(pl.BlockSpec((None, heads, dk, dv), lambda b, n: (b, 0, 0, 0)))
        args.append(s0)
    return pl.pallas_call(
        functools.partial(_gla_kernel, c=c, heads=heads, has_s0=s0 is not None, q_scale=float(dk) ** -0.5),
        out_shape=(jax.ShapeDtypeStruct((batch * seq, v_w), F32),
                   jax.ShapeDtypeStruct((batch, heads, dk, dv), F32)),
        grid=(batch, nchunk),
        in_specs=in_specs,
        out_specs=(pl.BlockSpec((c, v_w), lambda b, n: (b * nchunk + n, 0)),
                   pl.BlockSpec((None, heads, dk, dv), lambda b, n: (b, 0, 0, 0))),
        scratch_shapes=[pltpu.VMEM((heads, dk, dv), F32)],
        compiler_params=_params(("parallel", "arbitrary"), 40),
        name="gla_scan",
    )(*args)


def _two_group_specs(tm, width, n_first):
    t_first = n_first // tm
    return (pl.BlockSpec((tm, width), lambda i: (jnp.minimum(i, t_first - 1), 0)),
            pl.BlockSpec((tm, width), lambda i: (jnp.maximum(i - t_first, 0), 0)))


def _pick_group(first_ref, second_ref, n_first):
    in_first = pl.program_id(0) < n_first // first_ref.shape[0]
    return jnp.where(in_first, first_ref[...], second_ref[...])


def _gla_out_kernel(op_ref, os_ref, r_ref, x_ref, go_ref, w_ref, h_ref, *, heads, n_first):
    o = _pick_group(op_ref, os_ref, n_first)
    dv = o.shape[1] // heads
    go = go_ref[...]
    parts = [_rms(o[:, h * dv:(h + 1) * dv], go) for h in range(heads)]
    on = jnp.concatenate(parts, axis=1)
    r = r_ref[...]
    y = (on * (r * _sigmoid(r))).astype(BF16)
    h_ref[...] = x_ref[...] + jnp.dot(y, w_ref[...], preferred_element_type=F32)


def _gla_out(o_p, o_s, z, x, g_out, w_out, *, heads, tm):
    n, d = x.shape
    vw = o_p.shape[1]
    n_first = o_p.shape[0]
    rblk = (z.shape[1] - vw) // vw
    return pl.pallas_call(
        functools.partial(_gla_out_kernel, heads=heads, n_first=n_first),
        out_shape=jax.ShapeDtypeStruct((n, d), F32),
        grid=(n // tm,),
        in_specs=[
            *_two_group_specs(tm, vw, n_first),
            pl.BlockSpec((tm, vw), lambda i: (i, rblk)),
            pl.BlockSpec((tm, d), lambda i: (i, 0)),
            pl.BlockSpec((1, vw // heads), lambda i: (0, 0)),
            pl.BlockSpec((vw, d), lambda i: (0, 0)),
        ],
        out_specs=pl.BlockSpec((tm, d), lambda i: (i, 0)),
        compiler_params=_params(("parallel",), 44),
        name="gla_out",
    )(o_p, o_s, z, x, g_out, w_out)


def _from_slabs(ref, rows):
    slab = ref.shape[0] // rows
    return jnp.concatenate([ref[pl.ds(s, rows, stride=slab), :] for s in range(slab)], axis=1)


def _to_slabs(ref, val):
    rows = val.shape[0]
    slab = ref.shape[0] // rows
    for s in range(slab):
        ref[pl.ds(s, rows, stride=slab), :] = val[:, s * 128:(s + 1) * 128]


def _router_kernel(x_ref, g_ref, w_ref, xn_ref, info_ref):
    xn = _rms(x_ref[...], g_ref[...])
    _to_slabs(xn_ref, xn)
    lg = jnp.dot(xn, w_ref[...], precision=HIGHEST, preferred_element_type=F32)
    lane = lax.broadcasted_iota(jnp.int32, lg.shape, 1).astype(F32)
    far = float(lg.shape[1])
    is_grp = lane < N_EXPERT_GROUPS
    lgm = jnp.where(is_grp, lg, NEG_BIG)
    gmax = jnp.max(lgm, axis=-1, keepdims=True)
    gsum = jnp.sum(jnp.where(is_grp, jnp.exp(lg - gmax), 0.0), axis=-1, keepdims=True)
    p_top = 1.0 / gsum
    g_top = jnp.min(jnp.where(lgm == gmax, lane, far), axis=-1, keepdims=True)
    lo = N_EXPERT_GROUPS + g_top * EXPERTS_PER_GROUP
    in_grp = (lane >= lo) & (lane < lo + EXPERTS_PER_GROUP)
    le = jnp.where(in_grp, lg, NEG_BIG)
    m0 = jnp.max(le, axis=-1, keepdims=True)
    i0 = jnp.min(jnp.where(le == m0, lane, far), axis=-1, keepdims=True)
    le1 = jnp.where(lane == i0, NEG_BIG, le)
    m1 = jnp.max(le1, axis=-1, keepdims=True)
    i1 = jnp.min(jnp.where(le1 == m1, lane, far), axis=-1, keepdims=True)
    t = jnp.exp(m1 - m0)
    w0 = p_top / (1.0 + t)
    w1 = p_top * t / (1.0 + t)
    info = jnp.where(lane == 0.0, i0 - N_EXPERT_GROUPS,
                     jnp.where(lane == 1.0, i1 - N_EXPERT_GROUPS,
                               jnp.where(lane == 2.0, w0, jnp.where(lane == 3.0, w1, 0.0))))
    info_ref[...] = info


def _router(x, gain, w_router, *, tm):
    n, k = x.shape
    lanes = w_router.shape[1]
    slab = k // 128
    return pl.pallas_call(
        _router_kernel,
        out_shape=(jax.ShapeDtypeStruct((n * slab, 128), F32), jax.ShapeDtypeStruct((n, lanes), F32)),
        grid=(n // tm,),
        in_specs=[
            pl.BlockSpec((tm, k), lambda i: (i, 0)),
            pl.BlockSpec((1, k), lambda i: (0, 0)),
            pl.BlockSpec((k, lanes), lambda i: (0, 0)),
        ],
        out_specs=(pl.BlockSpec((tm * slab, 128), lambda i: (i, 0)), pl.BlockSpec((tm, lanes), lambda i: (i, 0))),
        compiler_params=_params(("parallel",), 32),
        name="moe_router",
    )(x, gain, w_router)


def _slab_copy(src_hbm, src_row, dst, dst_row, slab, sem):
    src_row = pl.multiple_of(src_row, slab)
    dst_row = pl.multiple_of(dst_row, slab)
    return pltpu.make_async_copy(src_hbm.at[pl.ds(src_row, slab)], dst.at[pl.ds(dst_row, slab)], sem)


def _start_slab_gather(src_hbm, idx_ref, base, dst, slab, sem):
    def body(r, carry):
        _slab_copy(src_hbm, idx_ref[base + r], dst, r * slab, slab, sem).start()
        return carry
    lax.fori_loop(0, dst.shape[0] // slab, body, 0, unroll=8)


def _wait_slab_gather(src_hbm, dst, slab, sem):
    def body(r, carry):
        _slab_copy(src_hbm, 0, dst, r * slab, slab, sem).wait()
        return carry
    lax.fori_loop(0, dst.shape[0] // slab, body, 0, unroll=8)


def _experts_kernel(te_ref, nu_ref, rt_ref, x_hbm, wgu_ref, wdn_ref, o_ref, xbuf, sem, wgu_bf, wdn_bf):
    t = pl.program_id(0)
    slab = wgu_ref.shape[0] // 128
    tm = xbuf.shape[1] // slab
    n_used = nu_ref[0]
    slot = lax.rem(t, 2)

    @pl.when((t == 0) & (n_used > 0))
    def _():
        _start_slab_gather(x_hbm, rt_ref, 0, xbuf.at[0], slab, sem.at[0])

    @pl.when(t + 1 < n_used)
    def _():
        _start_slab_gather(x_hbm, rt_ref, (t + 1) * tm, xbuf.at[1 - slot], slab, sem.at[1 - slot])

    @pl.when(t < n_used)
    def _():
        prev = te_ref[jnp.maximum(t - 1, 0)]

        @pl.when((t == 0) | (te_ref[t] != prev))
        def _():
            wgu_bf[...] = wgu_ref[...].astype(BF16)
            wdn_bf[...] = wdn_ref[...].astype(BF16)

        _wait_slab_gather(x_hbm, xbuf.at[slot], slab, sem.at[slot])
        x = _from_slabs(xbuf.at[slot], tm).astype(BF16)
        hg = jnp.dot(x, wgu_bf[...], preferred_element_type=F32)
        f = hg.shape[1] // 2
        a = hg[:, :f]
        hact = (a * _sigmoid(a)) * hg[:, f:]
        _to_slabs(o_ref, jnp.dot(hact.astype(BF16), wdn_bf[...], preferred_element_type=F32))

    @pl.when(t >= n_used)
    def _():
        o_ref[...] = jnp.zeros_like(o_ref)


def _experts(xn_slabs, row_start, tile_expert, n_used, w_gu, w_dn, *, layer, tm):
    p = row_start.shape[0]
    d, f2 = w_gu.shape[-2:]
    f = w_dn.shape[-2]
    slab = d // 128
    grid_spec = pltpu.PrefetchScalarGridSpec(
        num_scalar_prefetch=3,
        grid=(p // tm,),
        in_specs=[
            pl.BlockSpec(memory_space=pl.ANY),
            pl.BlockSpec((None, None, d, f2), lambda t, te, nu, rt: (layer, te[t], 0, 0)),
            pl.BlockSpec((None, None, f, d), lambda t, te, nu, rt: (layer, te[t], 0, 0)),
        ],
        out_specs=pl.BlockSpec((tm * slab, 128), lambda t, te, nu, rt: (t, 0)),
        scratch_shapes=[pltpu.VMEM((2, tm * slab, 128), F32), pltpu.SemaphoreType.DMA((2,)),
                        pltpu.VMEM((d, f2), BF16), pltpu.VMEM((f, d), BF16)],
    )
    return pl.pallas_call(
        _experts_kernel,
        out_shape=jax.ShapeDtypeStruct((p * slab, 128), F32),
        grid_spec=grid_spec,
        compiler_params=_params(("arbitrary",), 52),
        name="moe_experts",
    )(tile_expert, n_used, row_start, xn_slabs, w_gu, w_dn)


def _route(info, *, tm, slab):
    n = info.shape[0]
    e = info[:, :2].astype(jnp.int32).reshape(-1)
    onehot = (e[:, None] == jnp.arange(N_EXPERTS, dtype=jnp.int32)[None, :]).astype(jnp.int32)
    before = jnp.cumsum(onehot, axis=0) - onehot
    counts = jnp.sum(onehot, axis=0)
    padded = ((counts + tm - 1) // tm) * tm
    ends = jnp.cumsum(padded)
    starts = ends - padded
    pos = jnp.sum(onehot * (starts[None, :] + before), axis=1)
    p = ((2 * n + N_EXPERTS * (tm - 1)) // tm + 1) * tm
    row_start = jnp.zeros((p,), jnp.int32).at[pos].set((jnp.arange(2 * n, dtype=jnp.int32) // 2) * slab,
                                                       unique_indices=True, indices_are_sorted=False)
    tile_start = jnp.arange(p // tm, dtype=jnp.int32) * tm
    n_used = (ends[-1] // tm).astype(jnp.int32)
    te = jnp.sum((tile_start[:, None] >= ends[None, :]).astype(jnp.int32), axis=1)
    last = jnp.sum((jnp.maximum(ends[-1] - 1, 0) >= ends).astype(jnp.int32))
    te = jnp.where(tile_start < ends[-1], te, last).astype(jnp.int32)
    pos2 = (pos * slab).reshape(n, 2)
    return row_start, te, n_used.reshape(1), pos2[:, 0], pos2[:, 1]


def _ple_kernel(pos0_ref, pos1_ref, h_ref, info_ref, y_hbm, p_ref, gp_ref, wg_ref, wp_ref, *rest, final):
    if final:
        gf_ref, o_ref, ybuf, sem = rest
    else:
        o_ref, ybuf, sem = rest
    i = pl.program_id(0)
    tm, d = h_ref.shape
    slab = d // 128
    slot = lax.rem(i, 2)

    def start(tile, s):
        _start_slab_gather(y_hbm, pos0_ref, tile * tm, ybuf.at[s, 0], slab, sem.at[s])
        _start_slab_gather(y_hbm, pos1_ref, tile * tm, ybuf.at[s, 1], slab, sem.at[s])

    @pl.when(i == 0)
    def _():
        start(0, 0)

    @pl.when(i + 1 < pl.num_programs(0))
    def _():
        start(i + 1, 1 - slot)

    _wait_slab_gather(y_hbm, ybuf.at[slot, 0], slab, sem.at[slot])
    _wait_slab_gather(y_hbm, ybuf.at[slot, 1], slab, sem.at[slot])
    info = info_ref[...]
    y0 = _from_slabs(ybuf.at[slot, 0], tm)
    y1 = _from_slabs(ybuf.at[slot, 1], tm)
    h = h_ref[...] + (info[:, 2:3] * y0 + info[:, 3:4] * y1)
    hn = _rms(h, gp_ref[...]).astype(BF16)
    gate = _sigmoid(jnp.dot(hn, wg_ref[...], preferred_element_type=F32))
    proj = jnp.dot(p_ref[...].astype(BF16), wp_ref[...], preferred_element_type=F32)
    out = h + gate * proj
    if final:
        out = _rms(out, gf_ref[...])
    o_ref[...] = out


def _ple(h, info, y_sorted, pos0, pos1, *, p_emb, g_ple, w_gate, w_proj, g_final, tm, row0=0, n_rows=None):
    d = h.shape[1]
    n = h.shape[0] if n_rows is None else n_rows
    pd = p_emb.shape[1]
    r0 = row0 // tm
    row = lambda i, p0, p1: (r0 + i, 0)
    fix = lambda i, p0, p1: (0, 0)
    in_specs = [
        pl.BlockSpec((tm, d), row), pl.BlockSpec((tm, info.shape[1]), row), pl.BlockSpec(memory_space=pl.ANY),
        pl.BlockSpec((tm, pd), row), pl.BlockSpec((1, d), fix),
        pl.BlockSpec((d, d), fix), pl.BlockSpec((pd, d), fix),
    ]
    args = [h, info, y_sorted, p_emb, g_ple, w_gate, w_proj]
    if g_final is not None:
        in_specs.append(pl.BlockSpec((1, d), fix))
        args.append(g_final)
    grid_spec = pltpu.PrefetchScalarGridSpec(
        num_scalar_prefetch=2,
        grid=(n // tm,),
        in_specs=in_specs,
        out_specs=pl.BlockSpec((tm, d), lambda i, p0, p1: (i, 0)),
        scratch_shapes=[pltpu.VMEM((2, 2, tm * (d // 128), 128), F32), pltpu.SemaphoreType.DMA((2,))],
    )
    return pl.pallas_call(
        functools.partial(_ple_kernel, final=g_final is not None),
        out_shape=jax.ShapeDtypeStruct((n, d), F32),
        grid_spec=grid_spec,
        compiler_params=_params(("arbitrary",), 52),
        name="ple",
    )(pos0, pos1, *args)


def _linear_res_kernel(ap_ref, as_ref, w_ref, h_ref, o_ref, *, n_first):
    a = _pick_group(ap_ref, as_ref, n_first).astype(BF16)
    o_ref[...] = h_ref[...] + jnp.dot(a, w_ref[...], preferred_element_type=F32)


def _linear_res(a_p, a_s, w, h, *, tm):
    n, d = h.shape
    k = a_p.shape[1]
    n_first = a_p.shape[0]
    return pl.pallas_call(
        functools.partial(_linear_res_kernel, n_first=n_first),
        out_shape=jax.ShapeDtypeStruct((n, d), F32),
        grid=(n // tm,),
        in_specs=[*_two_group_specs(tm, k, n_first), pl.BlockSpec((k, d), lambda i: (0, 0)),
                  pl.BlockSpec((tm, d), lambda i: (i, 0))],
        out_specs=pl.BlockSpec((tm, d), lambda i: (i, 0)),
        compiler_params=_params(("parallel",), 40),
        name="attn_out",
    )(a_p, a_s, w, h)


def _band_attn_kernel(q0_ref, q1_ref, q2_ref, k_ref, v_ref, o_ref, og_ref, lse_ref, *, seq):
    q_refs = (q0_ref, q1_ref, q2_ref)
    row = lax.broadcasted_iota(jnp.int32, (BAND, BAND), 0)
    col = lax.broadcasted_iota(jnp.int32, (BAND, BAND), 1)
    cur_ok = col <= row
    prev_ok = col >= row
    row2 = lax.broadcasted_iota(jnp.int32, (BAND, 2 * BAND), 0)
    col2 = lax.broadcasted_iota(jnp.int32, (BAND, 2 * BAND), 1)
    both_ok = ((col2 < BAND) & (col2 >= row2)) | ((col2 >= BAND) & (col2 - BAND <= row2))
    del prev_ok
    nt = (((1,), (1,)), ((), ()))

    def rows(start, dil):
        return pl.ds(start, BAND) if dil == 1 else pl.ds(start, BAND, stride=dil)

    for gi, (win, dil) in enumerate(DIL_GROUPS):
        assert win // dil == BAND
        sub_len = seq // dil
        for r in range(dil):
            kp = vp = None
            for n in range(sub_len // BAND):
                sl = rows(r + dil * BAND * n, dil)
                qb = q_refs[gi][sl, :].astype(BF16)
                kc = k_ref[sl, :].astype(BF16)
                vc = v_ref[sl, :].astype(BF16)
                if kp is None:
                    keys, vals, ok = kc, vc, cur_ok
                else:
                    keys = jnp.concatenate([kp, kc], axis=0)
                    vals = jnp.concatenate([vp, vc], axis=0)
                    ok = both_ok
                s = jnp.where(ok, lax.dot_general(qb, keys, nt, preferred_element_type=F32), NEG_BIG)
                m = jnp.max(s, axis=-1, keepdims=True)
                p = jnp.exp(s - m)
                l = jnp.sum(p, axis=-1, keepdims=True)
                acc = jnp.dot(p.astype(BF16), vals, preferred_element_type=F32)
                og_ref[gi, sl, :] = acc / l
                lse_ref[gi, sl, :] = jnp.broadcast_to(m + jnp.log(l), (BAND, HEAD_DIM))
                kp, vp = kc, vc

    l0, l1, l2 = lse_ref[0], lse_ref[1], lse_ref[2]
    mx = jnp.maximum(jnp.maximum(l0, l1), l2)
    w0, w1, w2 = jnp.exp(l0 - mx), jnp.exp(l1 - mx), jnp.exp(l2 - mx)
    o_ref[...] = (w0 * og_ref[0] + w1 * og_ref[1] + w2 * og_ref[2]) / (w0 + w1 + w2)


def _band_attention(q, k, v, *, batch, seq, n_heads):
    blk = (seq, HEAD_DIM)
    qspec = lambda gi: pl.BlockSpec(blk, lambda b, h: (b, gi * n_heads + h))
    kv = pl.BlockSpec(blk, lambda b, h: (b, h))
    return pl.pallas_call(
        functools.partial(_band_attn_kernel, seq=seq),
        out_shape=jax.ShapeDtypeStruct((batch * seq, n_heads * HEAD_DIM), F32),
        grid=(batch, n_heads),
        in_specs=[qspec(0), qspec(1), qspec(2), kv, kv],
        out_specs=kv,
        scratch_shapes=[pltpu.VMEM((3, seq, HEAD_DIM), F32), pltpu.VMEM((3, seq, HEAD_DIM), F32)],
        compiler_params=_params(("parallel", "parallel"), 40),
        name="band_attention",
    )(q, q, q, k, v)


def _decode_rows(w_buf, dec_seq):
    max_dil = max(d for _, d in DIL_GROUPS)
    dense_from = w_buf
    for win, dil in DIL_GROUPS:
        if dil < max_dil:
            dense_from = min(dense_from, w_buf - win)
    dense_from = max((dense_from // max_dil) * max_dil, 0)
    return max_dil, dense_from


def _decode_bias(w_buf, dec_seq, key_pos, n_pad):
    bias = np.full((len(DIL_GROUPS) * dec_seq, n_pad), NEG_BIG, np.float32)
    for gi, (win, dil) in enumerate(DIL_GROUPS):
        for j in range(dec_seq):
            dist = (w_buf + j) - key_pos
            ok = (dist >= 0) & (dist <= win) & (dist % dil == 0)
            bias[gi * dec_seq + j, :len(key_pos)][ok] = 0.0
    return bias


def _decode_attn_kernel(q_ref, kn_ref, vn_ref, ka_ref, kb_ref, va_ref, vb_ref, bias_ref, o_ref, *,
                        n_heads, dec_seq, n_groups, n_pad):
    bias = bias_ref[...]
    nt = (((1,), (1,)), ((), ()))
    na = ka_ref.shape[0] * (ka_ref.shape[1] // n_heads)
    nb = kb_ref.shape[0] * (kb_ref.shape[1] // n_heads)
    tail = n_pad - na - nb - dec_seq

    def head_rows(a_ref, b_ref, n_ref, h):
        ra = a_ref.shape[1] // n_heads
        rb = b_ref.shape[1] // n_heads
        xa = a_ref[:, pl.ds(h, ra, stride=n_heads), :].reshape(na, HEAD_DIM)
        xb = b_ref[:, pl.ds(h, rb, stride=n_heads), :].reshape(nb, HEAD_DIM)
        xn = n_ref[:, h * HEAD_DIM:(h + 1) * HEAD_DIM]
        parts = [xa, xb, xn]
        if tail:
            parts.append(jnp.zeros((tail, HEAD_DIM), F32))
        return jnp.concatenate(parts, axis=0).astype(BF16)

    outs = []
    for h in range(n_heads):
        kh = head_rows(ka_ref, kb_ref, kn_ref, h)
        vh = head_rows(va_ref, vb_ref, vn_ref, h)
        qh = jnp.concatenate(
            [q_ref[:, (gi * n_heads + h) * HEAD_DIM:(gi * n_heads + h + 1) * HEAD_DIM] for gi in range(n_groups)],
            axis=0).astype(BF16)
        s = lax.dot_general(qh, kh, nt, preferred_element_type=F32) + bias
        m = jnp.max(s, axis=-1, keepdims=True)
        p = jnp.exp(s - m)
        l = jnp.sum(p, axis=-1, keepdims=True)
        og = jnp.dot(p.astype(BF16), vh, preferred_element_type=F32) / l
        lse = m + jnp.log(l)
        ls = [lse[gi * dec_seq:(gi + 1) * dec_seq] for gi in range(n_groups)]
        mx = functools.reduce(jnp.maximum, ls)
        ws = [jnp.exp(x - mx) for x in ls]
        num = sum(w * og[gi * dec_seq:(gi + 1) * dec_seq] for gi, w in enumerate(ws))
        outs.append(num / sum(ws))
    o_ref[...] = jnp.concatenate(outs, axis=1)


def _decode_attention(q, k_new, v_new, cache_k, cache_v, *, q_row0, n_heads):
    batch, w_buf = cache_k.shape[:2]
    dec_seq = k_new.shape[0] // batch
    n_groups = len(DIL_GROUPS)
    comb, dense_from = _decode_rows(w_buf, dec_seq)
    assert dec_seq <= comb and comb % dec_seq == 0 and dec_seq % 8 == 0 and w_buf % comb == 0
    n_comb = dense_from // comb
    n_dense = (w_buf - dense_from) // comb
    key_pos = np.concatenate([
        (np.arange(n_comb)[:, None] * comb + np.arange(dec_seq)[None, :]).reshape(-1),
        dense_from + np.arange(w_buf - dense_from),
        w_buf + np.arange(dec_seq)])
    n_pad = -(-len(key_pos) // 128) * 128
    bias = jnp.asarray(_decode_bias(w_buf, dec_seq, key_pos, n_pad))
    ck = cache_k.reshape(batch, w_buf // comb, comb * n_heads, HEAD_DIM)
    cv = cache_v.reshape(batch, w_buf // comb, comb * n_heads, HEAD_DIM)
    tok = lambda width, row0=0: pl.BlockSpec((dec_seq, width), lambda b: (row0 // dec_seq + b, 0))
    comb_spec = pl.BlockSpec((None, n_comb, dec_seq * n_heads, HEAD_DIM), lambda b: (b, 0, 0, 0))
    dense_spec = pl.BlockSpec((None, n_dense, comb * n_heads, HEAD_DIM), lambda b: (b, n_comb // n_dense, 0, 0))
    assert n_comb % n_dense == 0
    return pl.pallas_call(
        functools.partial(_decode_attn_kernel, n_heads=n_heads, dec_seq=dec_seq, n_groups=n_groups, n_pad=n_pad),
        out_shape=jax.ShapeDtypeStruct((batch * dec_seq, n_heads * HEAD_DIM), F32),
        grid=(batch,),
        in_specs=[tok(q.shape[1], q_row0), tok(k_new.shape[1]), tok(v_new.shape[1]),
                  comb_spec, dense_spec, comb_spec, dense_spec,
                  pl.BlockSpec(bias.shape, lambda b: (0, 0))],
        out_specs=pl.BlockSpec((dec_seq, n_heads * HEAD_DIM), lambda b: (b, 0)),
        compiler_params=_params(("parallel",), 48),
        name="decode_attention",
    )(q, k_new, v_new, ck, ck, cv, cv, bias)


def _moe_and_ple(h, p_emb, layer, w, g_final, *, tm_moe, split=None):
    n, d = h.shape
    lanes = 128
    w_router = jnp.zeros((d, lanes), F32)
    w_router = w_router.at[:, :N_EXPERT_GROUPS].set(w["moe_w_rg"][layer])
    w_router = w_router.at[:, N_EXPERT_GROUPS:N_EXPERT_GROUPS + N_EXPERTS].set(w["moe_w_re"][layer])
    xn, info = _router(h, w["norm_ffn"][layer][None, :], w_router, tm=512)
    row_start, te, n_used, pos0, pos1 = _route(info, tm=tm_moe, slab=d // 128)
    y_sorted = _experts(xn, row_start, te, n_used, w["moe_w_gu"], w["moe_w_dn"], layer=layer, tm=tm_moe)
    ple = functools.partial(_ple, h, info, y_sorted, p_emb=p_emb, g_ple=w["norm_ple"][layer][None, :],
                            w_gate=w["ple_w_gate"][layer].astype(BF16), w_proj=w["ple_w_proj"][layer].astype(BF16),
                            g_final=g_final, tm=256)
    if split is None:
        return ple(pos0, pos1)
    return (ple(pos0[:split], pos1[:split], row0=0, n_rows=split),
            ple(pos0[split:], pos1[split:], row0=split, n_rows=n - split))


def kernel(x_prompt, x_sample, p_prompt, p_sample, state_gla, cache_k, cache_v, norm_mix, norm_ffn, norm_ple,
           norm_kv, norm_final, gla_w_in, gla_w_a2, gla_b_a, gla_g_out, gla_w_out, w_kv, dil_w_q, dil_w_out,
           moe_w_rg, moe_w_re, moe_w_gu, moe_w_dn, ple_w_gate, ple_w_proj):
    w = dict(norm_ffn=norm_ffn, norm_ple=norm_ple, moe_w_rg=moe_w_rg, moe_w_re=moe_w_re, moe_w_gu=moe_w_gu,
             moe_w_dn=moe_w_dn, ple_w_gate=ple_w_gate, ple_w_proj=ple_w_proj)
    bp, tp, d = x_prompt.shape
    bs, ts, _ = x_sample.shape
    n_p, n_s = bp * tp, bs * ts
    depth = p_prompt.shape[0]
    assert depth == 2 and state_gla.shape[0] == 1
    heads, dk, dv = state_gla.shape[2:]
    n_kv = cache_k.shape[2]
    past_len = cache_k.shape[1]
    qk_w, v_w = heads * dk, heads * dv
    lowrank = gla_w_a2.shape[1]

    x = jnp.concatenate([x_prompt.reshape(n_p, d), x_sample.reshape(n_s, d)], axis=0)
    p_emb = jnp.concatenate([p_prompt.reshape(depth, n_p, -1), p_sample.reshape(depth, n_s, -1)], axis=1)

    half = HEAD_DIM // 2
    inv = ROPE_THETA ** (-jnp.arange(half, dtype=F32) / half)
    pos = jnp.concatenate([jnp.tile(jnp.arange(tp), bp), jnp.tile(past_len + jnp.arange(ts), bs)]).astype(F32)
    ang = pos[:, None] * inv[None, :]
    rope = (jnp.concatenate([jnp.cos(ang), jnp.cos(ang)], axis=1),
            jnp.concatenate([-jnp.sin(ang), jnp.sin(ang)], axis=1))

    g_mix0 = norm_mix[0][None, :]
    w_in = gla_w_in[0].astype(BF16)
    main_w = 2 * qk_w + 2 * v_w
    z = _norm_linear(x, g_mix0, w_in, col_start=0, n_cols=main_w, tn=1024, tm=512)
    w_a1 = jnp.zeros((d, 128), BF16).at[:, :lowrank].set(w_in[:, main_w:])
    w_a2 = jnp.zeros((128, qk_w), F32).at[:lowrank].set(gla_w_a2[0])
    g = _gla_gate(x, g_mix0, w_a1, w_a2, gla_b_a[0][None, :], tm=512)
    c_p = int(np.gcd(tp, GLA_CHUNK))
    c_s = int(np.gcd(ts, GLA_CHUNK))
    gla = functools.partial(_gla_scan, z, g, heads=heads, dk=dk, dv=dv)
    o_p, sg_p = gla(None, row0=0, batch=bp, seq=tp, c=c_p)
    o_s, sg_s = gla(state_gla[0], row0=n_p, batch=bs, seq=ts, c=c_s)
    h = _gla_out(o_p, o_s, z, x, gla_g_out[0][None, :], gla_w_out[0].astype(BF16), heads=heads, tm=256)
    h = _moe_and_ple(h, p_emb[0], 0, w, None, tm_moe=256)

    w_kv_b = w_kv.astype(BF16)
    kv_w = n_kv * HEAD_DIM
    g_kv = norm_kv[None, :]
    kv_lin = functools.partial(_norm_linear, h, g_kv, w_kv_b, n_cols=kv_w, tn=kv_w, tm=512)
    k_p = kv_lin(col_start=0, rope=rope, row0=0, n_rows=n_p)
    k_s = kv_lin(col_start=0, rope=rope, row0=n_p, n_rows=n_s)
    v_p = kv_lin(col_start=kv_w, row0=0, n_rows=n_p)
    v_s = kv_lin(col_start=kv_w, row0=n_p, n_rows=n_s)
    q_all = _norm_linear(h, norm_mix[1][None, :], dil_w_q[0].astype(BF16), col_start=0,
                         n_cols=dil_w_q.shape[2], tn=kv_w, tm=512, rope=rope, scale=HEAD_DIM ** -0.5)

    a_p = _band_attention(q_all, k_p, v_p, batch=bp, seq=tp, n_heads=n_kv)
    a_s = _decode_attention(q_all, k_s, v_s, cache_k, cache_v, q_row0=n_p, n_heads=n_kv)
    h = _linear_res(a_p, a_s, dil_w_out[0].astype(BF16), h, tm=512)
    y_p, y_s = _moe_and_ple(h, p_emb[1], 1, w, norm_final[None, :], tm_moe=256, split=n_p)

    rows = min(past_len, tp)
    as_heads = lambda a, b, t: a.reshape(b, t, n_kv, HEAD_DIM)
    return (y_p.reshape(bp, tp, d), y_s.reshape(bs, ts, d), sg_p[None], sg_s[None],
            as_heads(k_p, bp, tp)[:, tp - rows:], as_heads(v_p, bp, tp)[:, tp - rows:],
            as_heads(k_s, bs, ts), as_heads(v_s, bs, ts))
```

```python
import functools

import numpy as np
import jax
import jax.numpy as jnp
from jax import lax
from jax.experimental import pallas as pl
from jax.experimental.pallas import tpu as pltpu

F32 = jnp.float32
BF16 = jnp.bfloat16
HIGHEST = lax.Precision.HIGHEST

NORM_EPS = 1e-6
GLA_TAU = 16.0
GLA_CHUNK = 64
GLA_PAD = 128
HEAD_DIM = 128
BAND = 128
DIL_GROUPS = ((128, 1), (512, 4), (2048, 16))
ROPE_THETA = 10000.0
N_EXPERT_GROUPS = 4
EXPERTS_PER_GROUP = 8
N_EXPERTS = N_EXPERT_GROUPS * EXPERTS_PER_GROUP
NEG_BIG = -1e30
MIB = 1024 * 1024


def _params(semantics, vmem_mib):
    return pltpu.CompilerParams(dimension_semantics=semantics, vmem_limit_bytes=vmem_mib * MIB)


def _rms(x, gain):
    var = jnp.mean(x * x, axis=-1, keepdims=True)
    return x * lax.rsqrt(var + NORM_EPS) * gain


def _sigmoid(x):
    return 1.0 / (1.0 + jnp.exp(-x))


def _norm_linear_kernel(x_ref, g_ref, w_ref, *rest, rope, scale):
    if rope:
        cos_ref, sin_ref, o_ref, xn_ref = rest
    else:
        o_ref, xn_ref = rest

    @pl.when(pl.program_id(1) == 0)
    def _():
        xn_ref[...] = _rms(x_ref[...], g_ref[...]).astype(BF16)

    acc = jnp.dot(xn_ref[...], w_ref[...], preferred_element_type=F32)
    if rope:
        cos = cos_ref[...]
        sin = sin_ref[...]
        parts = []
        for c in range(acc.shape[1] // HEAD_DIM):
            y = acc[:, c * HEAD_DIM:(c + 1) * HEAD_DIM]
            parts.append(y * cos + pltpu.roll(y, HEAD_DIM // 2, 1) * sin)
        acc = parts[0] if len(parts) == 1 else jnp.concatenate(parts, axis=1)
    if scale != 1.0:
        acc = acc * scale
    o_ref[...] = acc


def _norm_linear(x, gain, w, *, col_start, n_cols, tn, tm, rope=None, scale=1.0, row0=0, n_rows=None):
    k = x.shape[1]
    n = x.shape[0] if n_rows is None else n_rows
    grid = (n // tm, n_cols // tn)
    c0 = col_start // tn
    r0 = row0 // tm
    in_specs = [
        pl.BlockSpec((tm, k), lambda i, j: (r0 + i, 0)),
        pl.BlockSpec((1, k), lambda i, j: (0, 0)),
        pl.BlockSpec((k, tn), lambda i, j: (0, c0 + j)),
    ]
    args = [x, gain, w]
    if rope is not None:
        in_specs += [pl.BlockSpec((tm, HEAD_DIM), lambda i, j: (r0 + i, 0))] * 2
        args += list(rope)
    return pl.pallas_call(
        functools.partial(_norm_linear_kernel, rope=rope is not None, scale=scale),
        out_shape=jax.ShapeDtypeStruct((n, n_cols), F32),
        grid=grid,
        in_specs=in_specs,
        out_specs=pl.BlockSpec((tm, tn), lambda i, j: (i, j)),
        scratch_shapes=[pltpu.VMEM((tm, k), BF16)],
        compiler_params=_params(("parallel", "arbitrary"), 48),
        name="norm_linear",
    )(*args)


def _gla_gate_kernel(x_ref, gn_ref, w1_ref, w2_ref, b_ref, o_ref):
    xn = _rms(x_ref[...], gn_ref[...]).astype(BF16)
    a1 = jnp.dot(xn, w1_ref[...], preferred_element_type=F32)
    pre = jnp.dot(a1, w2_ref[...], precision=HIGHEST, preferred_element_type=F32) + b_ref[...]
    o_ref[...] = (jnp.minimum(pre, 0.0) - jnp.log1p(jnp.exp(-jnp.abs(pre)))) * (1.0 / GLA_TAU)


def _gla_gate(x, gain, w1, w2, b_a, *, tm):
    n, k = x.shape
    r, qk = w2.shape
    return pl.pallas_call(
        _gla_gate_kernel,
        out_shape=jax.ShapeDtypeStruct((n, qk), F32),
        grid=(n // tm,),
        in_specs=[
            pl.BlockSpec((tm, k), lambda i: (i, 0)),
            pl.BlockSpec((1, k), lambda i: (0, 0)),
            pl.BlockSpec((k, r), lambda i: (0, 0)),
            pl.BlockSpec((r, qk), lambda i: (0, 0)),
            pl.BlockSpec((1, qk), lambda i: (0, 0)),
        ],
        out_specs=pl.BlockSpec((tm, qk), lambda i: (i, 0)),
        compiler_params=_params(("parallel",), 32),
        name="gla_gate",
    )(x, gain, w1, w2, b_a)


def _gla_kernel(*refs, c, heads, has_s0, q_scale):
    q_ref, k_ref, v_ref, g_ref = refs[:4]
    s0_ref = refs[4] if has_s0 else None
    o_ref, so_ref, s_scr = refs[-3:]
    n = pl.program_id(1)
    dk = q_ref.shape[1] // heads
    dv = v_ref.shape[1] // heads

    @pl.when(n == 0)
    def _():
        if has_s0:
            s_scr[...] = s0_ref[...]
        else:
            s_scr[...] = jnp.zeros_like(s_scr)

    def pad(a):
        if c == GLA_PAD:
            return a
        return jnp.concatenate([a, jnp.zeros((GLA_PAD - c, a.shape[1]), a.dtype)], axis=0)

    row = lax.broadcasted_iota(jnp.int32, (GLA_PAD, GLA_PAD), 0)
    col = lax.broadcasted_iota(jnp.int32, (GLA_PAD, GLA_PAD), 1)
    tri = jnp.where(row >= col, 1.0, 0.0).astype(F32)
    bp = jnp.dot(tri, pad(g_ref[...]), precision=HIGHEST, preferred_element_type=F32)
    b = bp[:c]
    k = k_ref[...]
    v = v_ref[...]
    qe = (q_ref[...] * q_scale * jnp.exp(b)).astype(BF16)
    ke = (k * jnp.exp(-b)).astype(BF16)
    vb = v.astype(BF16)
    k_t = pad(k).T
    b_t = bp.T
    b_last = b_t[:, c - 1:c]
    kd_t = (k_t * jnp.exp(b_last - b_t)).astype(BF16)
    decay = jnp.exp(b_last)
    vp = pad(v).astype(BF16)
    rc = lax.broadcasted_iota(jnp.int32, (c, c), 0)
    cc = lax.broadcasted_iota(jnp.int32, (c, c), 1)
    last = n == pl.num_programs(1) - 1
    for h in range(heads):
        ks = slice(h * dk, (h + 1) * dk)
        vs = slice(h * dv, (h + 1) * dv)
        s_old = s_scr[h]
        o = jnp.dot(qe[:, ks], s_old.astype(BF16), preferred_element_type=F32)
        a = lax.dot_general(qe[:, ks], ke[:, ks], (((1,), (1,)), ((), ())), preferred_element_type=F32)
        a = jnp.where(rc >= cc, a, 0.0)
        o_ref[:, vs] = o + jnp.dot(a.astype(BF16), vb[:, vs], preferred_element_type=F32)
        s_new = decay[ks] * s_old + jnp.dot(kd_t[ks], vp[:, vs], preferred_element_type=F32)
        s_scr[h] = s_new

        @pl.when(last)
        def _():
            so_ref[h] = s_new


def _gla_scan(z, g, s0, *, row0, batch, seq, heads, dk, dv, c):
    nchunk = seq // c
    rb0 = row0 // c
    qk_w, v_w = heads * dk, heads * dv
    rows = lambda b, n: rb0 + b * nchunk + n
    in_specs = [
        pl.BlockSpec((c, qk_w), lambda b, n: (rows(b, n), 0)),
        pl.BlockSpec((c, qk_w), lambda b, n: (rows(b, n), 1)),
        pl.BlockSpec((c, v_w), lambda b, n: (rows(b, n), 2 * qk_w // v_w)),
        pl.BlockSpec((c, qk_w), lambda b, n: (rows(b, n), 0)),
    ]
    args = [z, z, z, g]
    if s0 is not None:
        in_specs.append(pl.BlockSpec((None, heads, dk, dv), lambda b, n: (b, 0, 0, 0)))
        args.append(s0)
    return pl.pallas_call(
        functools.partial(_gla_kernel, c=c, heads=heads, has_s0=s0 is not None, q_scale=float(dk) ** -0.5),
        out_shape=(jax.ShapeDtypeStruct((batch * seq, v_w), F32),
                   jax.ShapeDtypeStruct((batch, heads, dk, dv), F32)),
        grid=(batch, nchunk),
        in_specs=in_specs,
        out_specs=(pl.BlockSpec((c, v_w), lambda b, n: (b * nchunk + n, 0)),
                   pl.BlockSpec((None, heads, dk, dv), lambda b, n: (b, 0, 0, 0))),
        scratch_shapes=[pltpu.VMEM((heads, dk, dv), F32)],
        compiler_params=_params(("parallel", "arbitrary"), 40),
        name="gla_scan",
    )(*args)


def _two_group_specs(tm, width, n_first):
    t_first = n_first // tm
    return (pl.BlockSpec((tm, width), lambda i: (jnp.minimum(i, t_first - 1), 0)),
            pl.BlockSpec((tm, width), lambda i: (jnp.maximum(i - t_first, 0), 0)))


def _pick_group(first_ref, second_ref, n_first):
    in_first = pl.program_id(0) < n_first // first_ref.shape[0]
    return jnp.where(in_first, first_ref[...], second_ref[...])


def _gla_out_kernel(op_ref, os_ref, r_ref, x_ref, go_ref, w_ref, h_ref, *, heads, n_first):
    o = _pick_group(op_ref, os_ref, n_first)
    dv = o.shape[1] // heads
    go = go_ref[...]
    parts = [_rms(o[:, h * dv:(h + 1) * dv], go) for h in range(heads)]
    on = jnp.concatenate(parts, axis=1)
    r = r_ref[...]
    y = (on * (r * _sigmoid(r))).astype(BF16)
    h_ref[...] = x_ref[...] + jnp.dot(y, w_ref[...], preferred_element_type=F32)


def _gla_out(o_p, o_s, z, x, g_out, w_out, *, heads, tm):
    n, d = x.shape
    vw = o_p.shape[1]
    n_first = o_p.shape[0]
    rblk = (z.shape[1] - vw) // vw
    return pl.pallas_call(
        functools.partial(_gla_out_kernel, heads=heads, n_first=n_first),
        out_shape=jax.ShapeDtypeStruct((n, d), F32),
        grid=(n // tm,),
        in_specs=[
            *_two_group_specs(tm, vw, n_first),
            pl.BlockSpec((tm, vw), lambda i: (i, rblk)),
            pl.BlockSpec((tm, d), lambda i: (i, 0)),
            pl.BlockSpec((1, vw // heads), lambda i: (0, 0)),
            pl.BlockSpec((vw, d), lambda i: (0, 0)),
        ],
        out_specs=pl.BlockSpec((tm, d), lambda i: (i, 0)),
        compiler_params=_params(("parallel",), 44),
        name="gla_out",
    )(o_p, o_s, z, x, g_out, w_out)


SLAB_WIDTH = 128
SLAB_DTYPE = F32


def _from_slabs(ref, rows):
    slab = ref.shape[0] // rows
    return jnp.concatenate([ref[pl.ds(s, rows, stride=slab), :] for s in range(slab)], axis=1)


def _to_slabs(ref, val):
    rows = val.shape[0]
    slab = ref.shape[0] // rows
    for s in range(slab):
        ref[pl.ds(s, rows, stride=slab), :] = val[:, s * 128:(s + 1) * 128]


def _router_kernel(x_ref, g_ref, w_ref, xn_ref, info_ref):
    xn = _rms(x_ref[...], g_ref[...])
    _to_slabs(xn_ref, xn)
    lg = jnp.dot(xn, w_ref[...], precision=HIGHEST, preferred_element_type=F32)
    lane = lax.broadcasted_iota(jnp.int32, lg.shape, 1).astype(F32)
    far = float(lg.shape[1])
    is_grp = lane < N_EXPERT_GROUPS
    lgm = jnp.where(is_grp, lg, NEG_BIG)
    gmax = jnp.max(lgm, axis=-1, keepdims=True)
    gsum = jnp.sum(jnp.where(is_grp, jnp.exp(lg - gmax), 0.0), axis=-1, keepdims=True)
    p_top = 1.0 / gsum
    g_top = jnp.min(jnp.where(lgm == gmax, lane, far), axis=-1, keepdims=True)
    lo = N_EXPERT_GROUPS + g_top * EXPERTS_PER_GROUP
    in_grp = (lane >= lo) & (lane < lo + EXPERTS_PER_GROUP)
    le = jnp.where(in_grp, lg, NEG_BIG)
    m0 = jnp.max(le, axis=-1, keepdims=True)
    i0 = jnp.min(jnp.where(le == m0, lane, far), axis=-1, keepdims=True)
    le1 = jnp.where(lane == i0, NEG_BIG, le)
    m1 = jnp.max(le1, axis=-1, keepdims=True)
    i1 = jnp.min(jnp.where(le1 == m1, lane, far), axis=-1, keepdims=True)
    t = jnp.exp(m1 - m0)
    w0 = p_top / (1.0 + t)
    w1 = p_top * t / (1.0 + t)
    info = jnp.where(lane == 0.0, i0 - N_EXPERT_GROUPS,
                     jnp.where(lane == 1.0, i1 - N_EXPERT_GROUPS,
                               jnp.where(lane == 2.0, w0, jnp.where(lane == 3.0, w1, 0.0))))
    info_ref[...] = info


def _router(x, gain, w_router, *, tm):
    n, k = x.shape
    lanes = w_router.shape[1]
    slab = k // SLAB_WIDTH
    return pl.pallas_call(
        _router_kernel,
        out_shape=(jax.ShapeDtypeStruct((n * slab, 128), SLAB_DTYPE), jax.ShapeDtypeStruct((n, lanes), F32)),
        grid=(n // tm,),
        in_specs=[
            pl.BlockSpec((tm, k), lambda i: (i, 0)),
            pl.BlockSpec((1, k), lambda i: (0, 0)),
            pl.BlockSpec((k, lanes), lambda i: (0, 0)),
        ],
        out_specs=(pl.BlockSpec((tm * slab, 128), lambda i: (i, 0)), pl.BlockSpec((tm, lanes), lambda i: (i, 0))),
        compiler_params=_params(("parallel",), 32),
        name="moe_router",
    )(x, gain, w_router)


def _slab_copy(src_hbm, src_row, dst, dst_row, slab, sem):
    src_row = pl.multiple_of(src_row, slab)
    dst_row = pl.multiple_of(dst_row, slab)
    return pltpu.make_async_copy(src_hbm.at[pl.ds(src_row, slab)], dst.at[pl.ds(dst_row, slab)], sem)


def _start_slab_gather(src_hbm, idx_ref, base, dst, slab, sem):
    def body(r, carry):
        _slab_copy(src_hbm, idx_ref[base + r], dst, r * slab, slab, sem).start()
        return carry
    lax.fori_loop(0, dst.shape[0] // slab, body, 0, unroll=8)


def _wait_slab_gather(src_hbm, dst, slab, sem):
    def body(r, carry):
        _slab_copy(src_hbm, 0, dst, r * slab, slab, sem).wait()
        return carry
    lax.fori_loop(0, dst.shape[0] // slab, body, 0, unroll=8)


def _expert_weight_copies(wgu_hbm, wdn_hbm, layer, expert, wgu_f32, wdn_f32, wslot, wsem):
    return (pltpu.make_async_copy(wgu_hbm.at[layer, expert], wgu_f32.at[wslot], wsem.at[wslot, 0]),
            pltpu.make_async_copy(wdn_hbm.at[layer, expert], wdn_f32.at[wslot], wsem.at[wslot, 1]))


def _experts_kernel(te_ref, nu_ref, rt_ref, first_ref, wslot_ref, next_ref, x_hbm, wgu_hbm, wdn_hbm, o_ref,
                    xbuf, sem, wgu_f32, wdn_f32, wsem, wgu_bf, wdn_bf, *, layer):
    t = pl.program_id(0)
    slab = wgu_bf.shape[0] // SLAB_WIDTH
    tm = xbuf.shape[1] // slab
    n_used = nu_ref[0]
    slot = lax.rem(t, 2)
    weights = functools.partial(_expert_weight_copies, wgu_hbm, wdn_hbm, layer,
                                wgu_f32=wgu_f32, wdn_f32=wdn_f32, wsem=wsem)

    @pl.when((t == 0) & (n_used > 0))
    def _():
        _start_slab_gather(x_hbm, rt_ref, 0, xbuf.at[0], slab, sem.at[0])
        for c in weights(te_ref[0], wslot=0):
            c.start()

    @pl.when(t + 1 < n_used)
    def _():
        _start_slab_gather(x_hbm, rt_ref, (t + 1) * tm, xbuf.at[1 - slot], slab, sem.at[1 - slot])

    @pl.when(t < n_used)
    def _():
        @pl.when(first_ref[t] == 1)
        def _():
            ws = wslot_ref[t]

            @pl.when(next_ref[t] >= 0)
            def _():
                for c in weights(next_ref[t], wslot=1 - ws):
                    c.start()

            for c in weights(te_ref[t], wslot=ws):
                c.wait()
            wgu_bf[...] = wgu_f32[ws].astype(BF16)
            wdn_bf[...] = wdn_f32[ws].astype(BF16)

        _wait_slab_gather(x_hbm, xbuf.at[slot], slab, sem.at[slot])
        x = _from_slabs(xbuf.at[slot], tm).astype(BF16)
        hg = jnp.dot(x, wgu_bf[...], preferred_element_type=F32)
        f = hg.shape[1] // 2
        a = hg[:, :f]
        hact = (a * _sigmoid(a)) * hg[:, f:]
        _to_slabs(o_ref, jnp.dot(hact.astype(BF16), wdn_bf[...], preferred_element_type=F32))

    @pl.when(t >= n_used)
    def _():
        o_ref[...] = jnp.zeros_like(o_ref)


def _experts(xn_slabs, plan, w_gu, w_dn, *, layer, tm):
    p = plan[2].shape[0]
    d, f2 = w_gu.shape[-2:]
    f = w_dn.shape[-2]
    slab = d // SLAB_WIDTH
    any_space = pl.BlockSpec(memory_space=pl.ANY)
    grid_spec = pltpu.PrefetchScalarGridSpec(
        num_scalar_prefetch=len(plan),
        grid=(p // tm,),
        in_specs=[any_space, any_space, any_space],
        out_specs=pl.BlockSpec((tm * slab, 128), lambda t, *_: (t, 0)),
        scratch_shapes=[pltpu.VMEM((2, tm * slab, 128), SLAB_DTYPE), pltpu.SemaphoreType.DMA((2,)),
                        pltpu.VMEM((2, d, f2), F32), pltpu.VMEM((2, f, d), F32), pltpu.SemaphoreType.DMA((2, 2)),
                        pltpu.VMEM((d, f2), BF16), pltpu.VMEM((f, d), BF16)],
    )
    return pl.pallas_call(
        functools.partial(_experts_kernel, layer=layer),
        out_shape=jax.ShapeDtypeStruct((p * slab, 128), SLAB_DTYPE),
        grid_spec=grid_spec,
        compiler_params=_params(("arbitrary",), 56),
        name="moe_experts",
    )(*plan, xn_slabs, w_gu, w_dn)


def _route(info, *, tm, slab):
    n = info.shape[0]
    e = info[:, :2].astype(jnp.int32).reshape(-1)
    onehot = (e[:, None] == jnp.arange(N_EXPERTS, dtype=jnp.int32)[None, :]).astype(jnp.int32)
    before = jnp.cumsum(onehot, axis=0) - onehot
    counts = jnp.sum(onehot, axis=0)
    padded = ((counts + tm - 1) // tm) * tm
    ends = jnp.cumsum(padded)
    starts = ends - padded
    pos = jnp.sum(onehot * (starts[None, :] + before), axis=1)
    p = ((2 * n + N_EXPERTS * (tm - 1)) // tm + 1) * tm
    row_start = jnp.zeros((p,), jnp.int32).at[pos].set((jnp.arange(2 * n, dtype=jnp.int32) // 2) * slab,
                                                       unique_indices=True, indices_are_sorted=False)
    tile_start = jnp.arange(p // tm, dtype=jnp.int32) * tm
    n_used = (ends[-1] // tm).astype(jnp.int32)
    used = tile_start < ends[-1]
    te = jnp.sum((tile_start[:, None] >= ends[None, :]).astype(jnp.int32), axis=1)
    last = jnp.sum((jnp.maximum(ends[-1] - 1, 0) >= ends).astype(jnp.int32))
    te = jnp.where(used, te, last).astype(jnp.int32)
    first = used & (te != jnp.concatenate([jnp.full((1,), -1, jnp.int32), te[:-1]]))
    wslot = lax.rem(jnp.cumsum(first.astype(jnp.int32)) - 1, 2)
    later = used[None, :] & (te[None, :] > te[:, None])
    nxt = jnp.min(jnp.where(later, te[None, :], N_EXPERTS), axis=1)
    nxt = jnp.where(nxt < N_EXPERTS, nxt, -1)
    plan = (te, n_used.reshape(1), row_start, first.astype(jnp.int32), wslot.astype(jnp.int32),
            nxt.astype(jnp.int32))
    pos2 = (pos * slab).reshape(n, 2)
    return plan, pos2[:, 0], pos2[:, 1]


def _ple_kernel(pos0_ref, pos1_ref, h_ref, info_ref, y_hbm, p_ref, gp_ref, wg_ref, wp_ref, *rest, final):
    if final:
        gf_ref, o_ref, ybuf, sem = rest
    else:
        o_ref, ybuf, sem = rest
    i = pl.program_id(0)
    tm, d = h_ref.shape
    slab = d // SLAB_WIDTH
    slot = lax.rem(i, 2)

    def start(tile, s):
        _start_slab_gather(y_hbm, pos0_ref, tile * tm, ybuf.at[s, 0], slab, sem.at[s])
        _start_slab_gather(y_hbm, pos1_ref, tile * tm, ybuf.at[s, 1], slab, sem.at[s])

    @pl.when(i == 0)
    def _():
        start(0, 0)

    @pl.when(i + 1 < pl.num_programs(0))
    def _():
        start(i + 1, 1 - slot)

    _wait_slab_gather(y_hbm, ybuf.at[slot, 0], slab, sem.at[slot])
    _wait_slab_gather(y_hbm, ybuf.at[slot, 1], slab, sem.at[slot])
    info = info_ref[...]
    y0 = _from_slabs(ybuf.at[slot, 0], tm)
    y1 = _from_slabs(ybuf.at[slot, 1], tm)
    h = h_ref[...] + (info[:, 2:3] * y0 + info[:, 3:4] * y1)
    hn = _rms(h, gp_ref[...]).astype(BF16)
    gate = _sigmoid(jnp.dot(hn, wg_ref[...], preferred_element_type=F32))
    proj = jnp.dot(p_ref[...].astype(BF16), wp_ref[...], preferred_element_type=F32)
    out = h + gate * proj
    if final:
        out = _rms(out, gf_ref[...])
    o_ref[...] = out


def _ple(h, info, y_sorted, pos0, pos1, *, p_emb, g_ple, w_gate, w_proj, g_final, tm, row0=0, n_rows=None):
    d = h.shape[1]
    n = h.shape[0] if n_rows is None else n_rows
    pd = p_emb.shape[1]
    r0 = row0 // tm
    row = lambda i, p0, p1: (r0 + i, 0)
    fix = lambda i, p0, p1: (0, 0)
    in_specs = [
        pl.BlockSpec((tm, d), row), pl.BlockSpec((tm, info.shape[1]), row), pl.BlockSpec(memory_space=pl.ANY),
        pl.BlockSpec((tm, pd), row), pl.BlockSpec((1, d), fix),
        pl.BlockSpec((d, d), fix), pl.BlockSpec((pd, d), fix),
    ]
    args = [h, info, y_sorted, p_emb, g_ple, w_gate, w_proj]
    if g_final is not None:
        in_specs.append(pl.BlockSpec((1, d), fix))
        args.append(g_final)
    grid_spec = pltpu.PrefetchScalarGridSpec(
        num_scalar_prefetch=2,
        grid=(n // tm,),
        in_specs=in_specs,
        out_specs=pl.BlockSpec((tm, d), lambda i, p0, p1: (i, 0)),
        scratch_shapes=[pltpu.VMEM((2, 2, tm * (d // SLAB_WIDTH), 128), SLAB_DTYPE), pltpu.SemaphoreType.DMA((2,))],
    )
    return pl.pallas_call(
        functools.partial(_ple_kernel, final=g_final is not None),
        out_shape=jax.ShapeDtypeStruct((n, d), F32),
        grid_spec=grid_spec,
        compiler_params=_params(("arbitrary",), 52),
        name="ple",
    )(pos0, pos1, *args)


def _linear_res_kernel(ap_ref, as_ref, w_ref, h_ref, o_ref, *, n_first):
    a = _pick_group(ap_ref, as_ref, n_first).astype(BF16)
    o_ref[...] = h_ref[...] + jnp.dot(a, w_ref[...], preferred_element_type=F32)


def _linear_res(a_p, a_s, w, h, *, tm):
    n, d = h.shape
    k = a_p.shape[1]
    n_first = a_p.shape[0]
    return pl.pallas_call(
        functools.partial(_linear_res_kernel, n_first=n_first),
        out_shape=jax.ShapeDtypeStruct((n, d), F32),
        grid=(n // tm,),
        in_specs=[*_two_group_specs(tm, k, n_first), pl.BlockSpec((k, d), lambda i: (0, 0)),
                  pl.BlockSpec((tm, d), lambda i: (i, 0))],
        out_specs=pl.BlockSpec((tm, d), lambda i: (i, 0)),
        compiler_params=_params(("parallel",), 40),
        name="attn_out",
    )(a_p, a_s, w, h)


def _band_attn_kernel(q0_ref, q1_ref, q2_ref, k_ref, v_ref, o_ref, og_ref, lse_ref, *, seq):
    q_refs = (q0_ref, q1_ref, q2_ref)
    row = lax.broadcasted_iota(jnp.int32, (BAND, BAND), 0)
    col = lax.broadcasted_iota(jnp.int32, (BAND, BAND), 1)
    cur_ok = col <= row
    prev_ok = col >= row
    row2 = lax.broadcasted_iota(jnp.int32, (BAND, 2 * BAND), 0)
    col2 = lax.broadcasted_iota(jnp.int32, (BAND, 2 * BAND), 1)
    both_ok = ((col2 < BAND) & (col2 >= row2)) | ((col2 >= BAND) & (col2 - BAND <= row2))
    del prev_ok
    nt = (((1,), (1,)), ((), ()))

    def rows(start, dil):
        return pl.ds(start, BAND) if dil == 1 else pl.ds(start, BAND, stride=dil)

    for gi, (win, dil) in enumerate(DIL_GROUPS):
        assert win // dil == BAND
        sub_len = seq // dil
        for r in range(dil):
            kp = vp = None
            for n in range(sub_len // BAND):
                sl = rows(r + dil * BAND * n, dil)
                qb = q_refs[gi][sl, :].astype(BF16)
                kc = k_ref[sl, :].astype(BF16)
                vc = v_ref[sl, :].astype(BF16)
                if kp is None:
                    keys, vals, ok = kc, vc, cur_ok
                else:
                    keys = jnp.concatenate([kp, kc], axis=0)
                    vals = jnp.concatenate([vp, vc], axis=0)
                    ok = both_ok
                s = jnp.where(ok, lax.dot_general(qb, keys, nt, preferred_element_type=F32), NEG_BIG)
                m = jnp.max(s, axis=-1, keepdims=True)
                p = jnp.exp(s - m)
                l = jnp.sum(p, axis=-1, keepdims=True)
                acc = jnp.dot(p.astype(BF16), vals, preferred_element_type=F32)
                og_ref[gi, sl, :] = acc / l
                lse_ref[gi, sl, :] = jnp.broadcast_to(m + jnp.log(l), (BAND, HEAD_DIM))
                kp, vp = kc, vc

    l0, l1, l2 = lse_ref[0], lse_ref[1], lse_ref[2]
    mx = jnp.maximum(jnp.maximum(l0, l1), l2)
    w0, w1, w2 = jnp.exp(l0 - mx), jnp.exp(l1 - mx), jnp.exp(l2 - mx)
    o_ref[...] = (w0 * og_ref[0] + w1 * og_ref[1] + w2 * og_ref[2]) / (w0 + w1 + w2)


def _band_attention(q, k, v, *, batch, seq, n_heads):
    blk = (seq, HEAD_DIM)
    qspec = lambda gi: pl.BlockSpec(blk, lambda b, h: (b, gi * n_heads + h))
    kv = pl.BlockSpec(blk, lambda b, h: (b, h))
    return pl.pallas_call(
        functools.partial(_band_attn_kernel, seq=seq),
        out_shape=jax.ShapeDtypeStruct((batch * seq, n_heads * HEAD_DIM), F32),
        grid=(batch, n_heads),
        in_specs=[qspec(0), qspec(1), qspec(2), kv, kv],
        out_specs=kv,
        scratch_shapes=[pltpu.VMEM((3, seq, HEAD_DIM), F32), pltpu.VMEM((3, seq, HEAD_DIM), F32)],
        compiler_params=_params(("parallel", "parallel"), 40),
        name="band_attention",
    )(q, q, q, k, v)


def _decode_rows(w_buf, dec_seq):
    max_dil = max(d for _, d in DIL_GROUPS)
    dense_from = w_buf
    for win, dil in DIL_GROUPS:
        if dil < max_dil:
            dense_from = min(dense_from, w_buf - win)
    dense_from = max((dense_from // max_dil) * max_dil, 0)
    return max_dil, dense_from


def _decode_bias(w_buf, dec_seq, key_pos, n_pad):
    bias = np.full((len(DIL_GROUPS) * dec_seq, n_pad), NEG_BIG, np.float32)
    for gi, (win, dil) in enumerate(DIL_GROUPS):
        for j in range(dec_seq):
            dist = (w_buf + j) - key_pos
            ok = (dist >= 0) & (dist <= win) & (dist % dil == 0)
            bias[gi * dec_seq + j, :len(key_pos)][ok] = 0.0
    return bias


def _decode_attn_kernel(q_ref, kn_ref, vn_ref, ka_ref, kb_ref, va_ref, vb_ref, bias_ref, o_ref, *,
                        n_heads, dec_seq, n_groups, n_pad):
    bias = bias_ref[...]
    nt = (((1,), (1,)), ((), ()))
    na = ka_ref.shape[0] * (ka_ref.shape[1] // n_heads)
    nb = kb_ref.shape[0] * (kb_ref.shape[1] // n_heads)
    tail = n_pad - na - nb - dec_seq

    def head_rows(a_ref, b_ref, n_ref, h):
        ra = a_ref.shape[1] // n_heads
        rb = b_ref.shape[1] // n_heads
        xa = a_ref[:, pl.ds(h, ra, stride=n_heads), :].reshape(na, HEAD_DIM)
        xb = b_ref[:, pl.ds(h, rb, stride=n_heads), :].reshape(nb, HEAD_DIM)
        xn = n_ref[:, h * HEAD_DIM:(h + 1) * HEAD_DIM]
        parts = [xa, xb, xn]
        if tail:
            parts.append(jnp.zeros((tail, HEAD_DIM), F32))
        return jnp.concatenate(parts, axis=0).astype(BF16)

    outs = []
    for h in range(n_heads):
        kh = head_rows(ka_ref, kb_ref, kn_ref, h)
        vh = head_rows(va_ref, vb_ref, vn_ref, h)
        qh = jnp.concatenate(
            [q_ref[:, (gi * n_heads + h) * HEAD_DIM:(gi * n_heads + h + 1) * HEAD_DIM] for gi in range(n_groups)],
            axis=0).astype(BF16)
        s = lax.dot_general(qh, kh, nt, preferred_element_type=F32) + bias
        m = jnp.max(s, axis=-1, keepdims=True)
        p = jnp.exp(s - m)
        l = jnp.sum(p, axis=-1, keepdims=True)
        og = jnp.dot(p.astype(BF16), vh, preferred_element_type=F32) / l
        lse = m + jnp.log(l)
        ls = [lse[gi * dec_seq:(gi + 1) * dec_seq] for gi in range(n_groups)]
        mx = functools.reduce(jnp.maximum, ls)
        ws = [jnp.exp(x - mx) for x in ls]
        num = sum(w * og[gi * dec_seq:(gi + 1) * dec_seq] for gi, w in enumerate(ws))
        outs.append(num / sum(ws))
    o_ref[...] = jnp.concatenate(outs, axis=1)


def _decode_attention(q, k_new, v_new, cache_k, cache_v, *, q_row0, n_heads):
    batch, w_buf = cache_k.shape[:2]
    dec_seq = k_new.shape[0] // batch
    n_groups = len(DIL_GROUPS)
    comb, dense_from = _decode_rows(w_buf, dec_seq)
    assert dec_seq <= comb and comb % dec_seq == 0 and dec_seq % 8 == 0 and w_buf % comb == 0
    n_comb = dense_from // comb
    n_dense = (w_buf - dense_from) // comb
    key_pos = np.concatenate([
        (np.arange(n_comb)[:, None] * comb + np.arange(dec_seq)[None, :]).reshape(-1),
        dense_from + np.arange(w_buf - dense_from),
        w_buf + np.arange(dec_seq)])
    n_pad = -(-len(key_pos) // 128) * 128
    bias = jnp.asarray(_decode_bias(w_buf, dec_seq, key_pos, n_pad))
    ck = cache_k.reshape(batch, w_buf // comb, comb * n_heads, HEAD_DIM)
    cv = cache_v.reshape(batch, w_buf // comb, comb * n_heads, HEAD_DIM)
    tok = lambda width, row0=0: pl.BlockSpec((dec_seq, width), lambda b: (row0 // dec_seq + b, 0))
    comb_spec = pl.BlockSpec((None, n_comb, dec_seq * n_heads, HEAD_DIM), lambda b: (b, 0, 0, 0))
    dense_spec = pl.BlockSpec((None, n_dense, comb * n_heads, HEAD_DIM), lambda b: (b, n_comb // n_dense, 0, 0))
    assert n_comb % n_dense == 0
    return pl.pallas_call(
        functools.partial(_decode_attn_kernel, n_heads=n_heads, dec_seq=dec_seq, n_groups=n_groups, n_pad=n_pad),
        out_shape=jax.ShapeDtypeStruct((batch * dec_seq, n_heads * HEAD_DIM), F32),
        grid=(batch,),
        in_specs=[tok(q.shape[1], q_row0), tok(k_new.shape[1]), tok(v_new.shape[1]),
                  comb_spec, dense_spec, comb_spec, dense_spec,
                  pl.BlockSpec(bias.shape, lambda b: (0, 0))],
        out_specs=pl.BlockSpec((dec_seq, n_heads * HEAD_DIM), lambda b: (b, 0)),
        compiler_params=_params(("parallel",), 48),
        name="decode_attention",
    )(q, k_new, v_new, ck, ck, cv, cv, bias)


def _moe_and_ple(h, p_emb, layer, w, g_final, *, tm_moe, split=None):
    n, d = h.shape
    lanes = 128
    w_router = jnp.zeros((d, lanes), F32)
    w_router = w_router.at[:, :N_EXPERT_GROUPS].set(w["moe_w_rg"][layer])
    w_router = w_router.at[:, N_EXPERT_GROUPS:N_EXPERT_GROUPS + N_EXPERTS].set(w["moe_w_re"][layer])
    xn, info = _router(h, w["norm_ffn"][layer][None, :], w_router, tm=512)
    plan, pos0, pos1 = _route(info, tm=tm_moe, slab=d // SLAB_WIDTH)
    y_sorted = _experts(xn, plan, w["moe_w_gu"], w["moe_w_dn"], layer=layer, tm=tm_moe)
    ple = functools.partial(_ple, h, info, y_sorted, p_emb=p_emb, g_ple=w["norm_ple"][layer][None, :],
                            w_gate=w["ple_w_gate"][layer].astype(BF16), w_proj=w["ple_w_proj"][layer].astype(BF16),
                            g_final=g_final, tm=256)
    if split is None:
        return ple(pos0, pos1)
    return (ple(pos0[:split], pos1[:split], row0=0, n_rows=split),
            ple(pos0[split:], pos1[split:], row0=split, n_rows=n - split))


def kernel(x_prompt, x_sample, p_prompt, p_sample, state_gla, cache_k, cache_v, norm_mix, norm_ffn, norm_ple,
           norm_kv, norm_final, gla_w_in, gla_w_a2, gla_b_a, gla_g_out, gla_w_out, w_kv, dil_w_q, dil_w_out,
           moe_w_rg, moe_w_re, moe_w_gu, moe_w_dn, ple_w_gate, ple_w_proj):
    w = dict(norm_ffn=norm_ffn, norm_ple=norm_ple, moe_w_rg=moe_w_rg, moe_w_re=moe_w_re, moe_w_gu=moe_w_gu,
             moe_w_dn=moe_w_dn, ple_w_gate=ple_w_gate, ple_w_proj=ple_w_proj)
    bp, tp, d = x_prompt.shape
    bs, ts, _ = x_sample.shape
    n_p, n_s = bp * tp, bs * ts
    depth = p_prompt.shape[0]
    assert depth == 2 and state_gla.shape[0] == 1
    heads, dk, dv = state_gla.shape[2:]
    n_kv = cache_k.shape[2]
    past_len = cache_k.shape[1]
    qk_w, v_w = heads * dk, heads * dv
    lowrank = gla_w_a2.shape[1]

    x = jnp.concatenate([x_prompt.reshape(n_p, d), x_sample.reshape(n_s, d)], axis=0)
    p_emb = jnp.concatenate([p_prompt.reshape(depth, n_p, -1), p_sample.reshape(depth, n_s, -1)], axis=1)

    half = HEAD_DIM // 2
    inv = ROPE_THETA ** (-jnp.arange(half, dtype=F32) / half)
    pos = jnp.concatenate([jnp.tile(jnp.arange(tp), bp), jnp.tile(past_len + jnp.arange(ts), bs)]).astype(F32)
    ang = pos[:, None] * inv[None, :]
    rope = (jnp.concatenate([jnp.cos(ang), jnp.cos(ang)], axis=1),
            jnp.concatenate([-jnp.sin(ang), jnp.sin(ang)], axis=1))

    g_mix0 = norm_mix[0][None, :]
    w_in = gla_w_in[0].astype(BF16)
    main_w = 2 * qk_w + 2 * v_w
    tm_wide = 1024 if (n_p + n_s) % 1024 == 0 else 512
    z = _norm_linear(x, g_mix0, w_in, col_start=0, n_cols=main_w, tn=1024, tm=tm_wide)
    w_a1 = jnp.zeros((d, 128), BF16).at[:, :lowrank].set(w_in[:, main_w:])
    w_a2 = jnp.zeros((128, qk_w), F32).at[:lowrank].set(gla_w_a2[0])
    g = _gla_gate(x, g_mix0, w_a1, w_a2, gla_b_a[0][None, :], tm=512)
    c_p = int(np.gcd(tp, GLA_CHUNK))
    c_s = int(np.gcd(ts, GLA_CHUNK))
    gla = functools.partial(_gla_scan, z, g, heads=heads, dk=dk, dv=dv)
    o_p, sg_p = gla(None, row0=0, batch=bp, seq=tp, c=c_p)
    o_s, sg_s = gla(state_gla[0], row0=n_p, batch=bs, seq=ts, c=c_s)
    h = _gla_out(o_p, o_s, z, x, gla_g_out[0][None, :], gla_w_out[0].astype(BF16), heads=heads, tm=256)
    h = _moe_and_ple(h, p_emb[0], 0, w, None, tm_moe=256)

    w_kv_b = w_kv.astype(BF16)
    kv_w = n_kv * HEAD_DIM
    g_kv = norm_kv[None, :]
    kv_lin = functools.partial(_norm_linear, h, g_kv, w_kv_b, n_cols=kv_w, tn=kv_w, tm=512)
    k_p = kv_lin(col_start=0, rope=rope, row0=0, n_rows=n_p)
    k_s = kv_lin(col_start=0, rope=rope, row0=n_p, n_rows=n_s)
    v_p = kv_lin(col_start=kv_w, row0=0, n_rows=n_p)
    v_s = kv_lin(col_start=kv_w, row0=n_p, n_rows=n_s)
    q_all = _norm_linear(h, norm_mix[1][None, :], dil_w_q[0].astype(BF16), col_start=0,
                         n_cols=dil_w_q.shape[2], tn=kv_w, tm=tm_wide, rope=rope, scale=HEAD_DIM ** -0.5)

    a_p = _band_attention(q_all, k_p, v_p, batch=bp, seq=tp, n_heads=n_kv)
    a_s = _decode_attention(q_all, k_s, v_s, cache_k, cache_v, q_row0=n_p, n_heads=n_kv)
    h = _linear_res(a_p, a_s, dil_w_out[0].astype(BF16), h, tm=512)
    y_p, y_s = _moe_and_ple(h, p_emb[1], 1, w, norm_final[None, :], tm_moe=256, split=n_p)

    rows = min(past_len, tp)
    as_heads = lambda a, b, t: a.reshape(b, t, n_kv, HEAD_DIM)
    return (y_p.reshape(bp, tp, d), y_s.reshape(bs, ts, d), sg_p[None], sg_s[None],
            as_heads(k_p, bp, tp)[:, tp - rows:], as_heads(v_p, bp, tp)[:, tp - rows:],
            as_heads(k_s, bs, ts), as_heads(v_s, bs, ts))
```

```python
import functools

import numpy as np
import jax
import jax.numpy as jnp
from jax import lax
from jax.experimental import pallas as pl
from jax.experimental.pallas import tpu as pltpu

F32 = jnp.float32
BF16 = jnp.bfloat16
HIGHEST = lax.Precision.HIGHEST

NORM_EPS = 1e-6
GLA_TAU = 16.0
GLA_CHUNK = 64
GLA_PAD = 128
HEAD_DIM = 128
BAND = 128
DIL_GROUPS = ((128, 1), (512, 4), (2048, 16))
ROPE_THETA = 10000.0
N_EXPERT_GROUPS = 4
EXPERTS_PER_GROUP = 8
N_EXPERTS = N_EXPERT_GROUPS * EXPERTS_PER_GROUP
NEG_BIG = -1e30
MIB = 1024 * 1024


def _params(semantics, vmem_mib):
    return pltpu.CompilerParams(dimension_semantics=semantics, vmem_limit_bytes=vmem_mib * MIB)


def _rms(x, gain):
    var = jnp.mean(x * x, axis=-1, keepdims=True)
    return x * lax.rsqrt(var + NORM_EPS) * gain


def _sigmoid(x):
    return 1.0 / (1.0 + jnp.exp(-x))


def _norm_linear_kernel(x_ref, g_ref, w_ref, *rest, rope, scale):
    if rope:
        cos_ref, sin_ref, o_ref, xn_ref = rest
    else:
        o_ref, xn_ref = rest

    @pl.when(pl.program_id(1) == 0)
    def _():
        xn_ref[...] = _rms(x_ref[...], g_ref[...]).astype(BF16)

    acc = jnp.dot(xn_ref[...], w_ref[...], preferred_element_type=F32)
    if rope:
        cos = cos_ref[...]
        sin = sin_ref[...]
        parts = []
        for c in range(acc.shape[1] // HEAD_DIM):
            y = acc[:, c * HEAD_DIM:(c + 1) * HEAD_DIM]
            parts.append(y * cos + pltpu.roll(y, HEAD_DIM // 2, 1) * sin)
        acc = parts[0] if len(parts) == 1 else jnp.concatenate(parts, axis=1)
    if scale != 1.0:
        acc = acc * scale
    o_ref[...] = acc


def _norm_linear(x, gain, w, *, col_start, n_cols, tn, tm, rope=None, scale=1.0, row0=0, n_rows=None):
    k = x.shape[1]
    n = x.shape[0] if n_rows is None else n_rows
    grid = (n // tm, n_cols // tn)
    c0 = col_start // tn
    r0 = row0 // tm
    in_specs = [
        pl.BlockSpec((tm, k), lambda i, j: (r0 + i, 0)),
        pl.BlockSpec((1, k), lambda i, j: (0, 0)),
        pl.BlockSpec((k, tn), lambda i, j: (0, c0 + j)),
    ]
    args = [x, gain, w]
    if rope is not None:
        in_specs += [pl.BlockSpec((tm, HEAD_DIM), lambda i, j: (r0 + i, 0))] * 2
        args += list(rope)
    return pl.pallas_call(
        functools.partial(_norm_linear_kernel, rope=rope is not None, scale=scale),
        out_shape=jax.ShapeDtypeStruct((n, n_cols), F32),
        grid=grid,
        in_specs=in_specs,
        out_specs=pl.BlockSpec((tm, tn), lambda i, j: (i, j)),
        scratch_shapes=[pltpu.VMEM((tm, k), BF16)],
        compiler_params=_params(("parallel", "arbitrary"), 48),
        name="norm_linear",
    )(*args)


def _gla_gate_kernel(x_ref, gn_ref, w1_ref, w2_ref, b_ref, o_ref):
    xn = _rms(x_ref[...], gn_ref[...]).astype(BF16)
    a1 = jnp.dot(xn, w1_ref[...], preferred_element_type=F32)
    pre = jnp.dot(a1, w2_ref[...], precision=HIGHEST, preferred_element_type=F32) + b_ref[...]
    o_ref[...] = (jnp.minimum(pre, 0.0) - jnp.log1p(jnp.exp(-jnp.abs(pre)))) * (1.0 / GLA_TAU)


def _gla_gate(x, gain, w1, w2, b_a, *, tm):
    n, k = x.shape
    r, qk = w2.shape
    return pl.pallas_call(
        _gla_gate_kernel,
        out_shape=jax.ShapeDtypeStruct((n, qk), F32),
        grid=(n // tm,),
        in_specs=[
            pl.BlockSpec((tm, k), lambda i: (i, 0)),
            pl.BlockSpec((1, k), lambda i: (0, 0)),
            pl.BlockSpec((k, r), lambda i: (0, 0)),
            pl.BlockSpec((r, qk), lambda i: (0, 0)),
            pl.BlockSpec((1, qk), lambda i: (0, 0)),
        ],
        out_specs=pl.BlockSpec((tm, qk), lambda i: (i, 0)),
        compiler_params=_params(("parallel",), 32),
        name="gla_gate",
    )(x, gain, w1, w2, b_a)


def _gla_kernel(*refs, c, heads, has_s0, q_scale):
    q_ref, k_ref, v_ref, g_ref = refs[:4]
    s0_ref = refs[4] if has_s0 else None
    o_ref, so_ref, s_scr = refs[-3:]
    n = pl.program_id(1)
    dk = q_ref.shape[1] // heads
    dv = v_ref.shape[1] // heads

    @pl.when(n == 0)
    def _():
        if has_s0:
            s_scr[...] = s0_ref[...]
        else:
            s_scr[...] = jnp.zeros_like(s_scr)

    def pad(a):
        if c == GLA_PAD:
            return a
        return jnp.concatenate([a, jnp.zeros((GLA_PAD - c, a.shape[1]), a.dtype)], axis=0)

    row = lax.broadcasted_iota(jnp.int32, (GLA_PAD, GLA_PAD), 0)
    col = lax.broadcasted_iota(jnp.int32, (GLA_PAD, GLA_PAD), 1)
    tri = jnp.where(row >= col, 1.0, 0.0).astype(F32)
    bp = jnp.dot(tri, pad(g_ref[...]), precision=HIGHEST, preferred_element_type=F32)
    b = bp[:c]
    k = k_ref[...]
    v = v_ref[...]
    qe = (q_ref[...] * q_scale * jnp.exp(b)).astype(BF16)
    ke = (k * jnp.exp(-b)).astype(BF16)
    vb = v.astype(BF16)
    k_t = pad(k).T
    b_t = bp.T
    b_last = b_t[:, c - 1:c]
    kd_t = (k_t * jnp.exp(b_last - b_t)).astype(BF16)
    decay = jnp.exp(b_last)
    vp = pad(v).astype(BF16)
    rc = lax.broadcasted_iota(jnp.int32, (c, c), 0)
    cc = lax.broadcasted_iota(jnp.int32, (c, c), 1)
    last = n == pl.num_programs(1) - 1
    for h in range(heads):
        ks = slice(h * dk, (h + 1) * dk)
        vs = slice(h * dv, (h + 1) * dv)
        s_old = s_scr[h]
        o = jnp.dot(qe[:, ks], s_old.astype(BF16), preferred_element_type=F32)
        a = lax.dot_general(qe[:, ks], ke[:, ks], (((1,), (1,)), ((), ())), preferred_element_type=F32)
        a = jnp.where(rc >= cc, a, 0.0)
        o_ref[:, vs] = o + jnp.dot(a.astype(BF16), vb[:, vs], preferred_element_type=F32)
        s_new = decay[ks] * s_old + jnp.dot(kd_t[ks], vp[:, vs], preferred_element_type=F32)
        s_scr[h] = s_new

        @pl.when(last)
        def _():
            so_ref[h] = s_new


def _gla_scan(z, g, s0, *, row0, batch, seq, heads, dk, dv, c):
    nchunk = seq // c
    rb0 = row0 // c
    qk_w, v_w = heads * dk, heads * dv
    rows = lambda b, n: rb0 + b * nchunk + n
    in_specs = [
        pl.BlockSpec((c, qk_w), lambda b, n: (rows(b, n), 0)),
        pl.BlockSpec((c, qk_w), lambda b, n: (rows(b, n), 1)),
        pl.BlockSpec((c, v_w), lambda b, n: (rows(b, n), 2 * qk_w // v_w)),
        pl.BlockSpec((c, qk_w), lambda b, n: (rows(b, n), 0)),
    ]
    args = [z, z, z, g]
    if s0 is not None:
        in_specs.append(pl.BlockSpec((None, heads, dk, dv), lambda b, n: (b, 0, 0, 0)))
        args.append(s0)
    return pl.pallas_call(
        functools.partial(_gla_kernel, c=c, heads=heads, has_s0=s0 is not None, q_scale=float(dk) ** -0.5),
        out_shape=(jax.ShapeDtypeStruct((batch * seq, v_w), F32),
                   jax.ShapeDtypeStruct((batch, heads, dk, dv), F32)),
        grid=(batch, nchunk),
        in_specs=in_specs,
        out_specs=(pl.BlockSpec((c, v_w), lambda b, n: (b * nchunk + n, 0)),
                   pl.BlockSpec((None, heads, dk, dv), lambda b, n: (b, 0, 0, 0))),
        scratch_shapes=[pltpu.VMEM((heads, dk, dv), F32)],
        compiler_params=_params(("parallel", "arbitrary"), 40),
        name="gla_scan",
    )(*args)


def _two_group_specs(tm, width, n_first):
    t_first = n_first // tm
    return (pl.BlockSpec((tm, width), lambda i: (jnp.minimum(i, t_first - 1), 0)),
            pl.BlockSpec((tm, width), lambda i: (jnp.maximum(i - t_first, 0), 0)))


def _pick_group(first_ref, second_ref, n_first):
    in_first = pl.program_id(0) < n_first // first_ref.shape[0]
    return jnp.where(in_first, first_ref[...], second_ref[...])


def _gla_out_kernel(op_ref, os_ref, r_ref, x_ref, go_ref, w_ref, h_ref, *, heads, n_first):
    o = _pick_group(op_ref, os_ref, n_first)
    dv = o.shape[1] // heads
    go = go_ref[...]
    parts = [_rms(o[:, h * dv:(h + 1) * dv], go) for h in range(heads)]
    on = jnp.concatenate(parts, axis=1)
    r = r_ref[...]
    y = (on * (r * _sigmoid(r))).astype(BF16)
    h_ref[...] = x_ref[...] + jnp.dot(y, w_ref[...], preferred_element_type=F32)


def _gla_out(o_p, o_s, z, x, g_out, w_out, *, heads, tm):
    n, d = x.shape
    vw = o_p.shape[1]
    n_first = o_p.shape[0]
    rblk = (z.shape[1] - vw) // vw
    return pl.pallas_call(
        functools.partial(_gla_out_kernel, heads=heads, n_first=n_first),
        out_shape=jax.ShapeDtypeStruct((n, d), F32),
        grid=(n // tm,),
        in_specs=[
            *_two_group_specs(tm, vw, n_first),
            pl.BlockSpec((tm, vw), lambda i: (i, rblk)),
            pl.BlockSpec((tm, d), lambda i: (i, 0)),
            pl.BlockSpec((1, vw // heads), lambda i: (0, 0)),
            pl.BlockSpec((vw, d), lambda i: (0, 0)),
        ],
        out_specs=pl.BlockSpec((tm, d), lambda i: (i, 0)),
        compiler_params=_params(("parallel",), 44),
        name="gla_out",
    )(o_p, o_s, z, x, g_out, w_out)


SLAB_WIDTH = 128
SLAB_DTYPE = F32
MOE_COL_CHUNKS = 4


def _from_slabs(ref, rows):
    slab = ref.shape[0] // rows
    return jnp.concatenate([ref[pl.ds(s, rows, stride=slab), :] for s in range(slab)], axis=1)


def _to_slabs(ref, val, first_row=0):
    rows = val.shape[0]
    slab = ref.shape[0] // rows
    for s in range(val.shape[1] // SLAB_WIDTH):
        ref[pl.ds(first_row + s, rows, stride=slab), :] = val[:, s * SLAB_WIDTH:(s + 1) * SLAB_WIDTH]


def _router_kernel(x_ref, g_ref, w_ref, xn_ref, info_ref):
    xn = _rms(x_ref[...], g_ref[...])
    _to_slabs(xn_ref, xn)
    lg = jnp.dot(xn, w_ref[...], precision=HIGHEST, preferred_element_type=F32)
    lane = lax.broadcasted_iota(jnp.int32, lg.shape, 1).astype(F32)
    far = float(lg.shape[1])
    is_grp = lane < N_EXPERT_GROUPS
    lgm = jnp.where(is_grp, lg, NEG_BIG)
    gmax = jnp.max(lgm, axis=-1, keepdims=True)
    gsum = jnp.sum(jnp.where(is_grp, jnp.exp(lg - gmax), 0.0), axis=-1, keepdims=True)
    p_top = 1.0 / gsum
    g_top = jnp.min(jnp.where(lgm == gmax, lane, far), axis=-1, keepdims=True)
    lo = N_EXPERT_GROUPS + g_top * EXPERTS_PER_GROUP
    in_grp = (lane >= lo) & (lane < lo + EXPERTS_PER_GROUP)
    le = jnp.where(in_grp, lg, NEG_BIG)
    m0 = jnp.max(le, axis=-1, keepdims=True)
    i0 = jnp.min(jnp.where(le == m0, lane, far), axis=-1, keepdims=True)
    le1 = jnp.where(lane == i0, NEG_BIG, le)
    m1 = jnp.max(le1, axis=-1, keepdims=True)
    i1 = jnp.min(jnp.where(le1 == m1, lane, far), axis=-1, keepdims=True)
    t = jnp.exp(m1 - m0)
    w0 = p_top / (1.0 + t)
    w1 = p_top * t / (1.0 + t)
    info = jnp.where(lane == 0.0, i0 - N_EXPERT_GROUPS,
                     jnp.where(lane == 1.0, i1 - N_EXPERT_GROUPS,
                               jnp.where(lane == 2.0, w0, jnp.where(lane == 3.0, w1, 0.0))))
    info_ref[...] = info


def _router(x, gain, w_router, *, tm):
    n, k = x.shape
    lanes = w_router.shape[1]
    slab = k // SLAB_WIDTH
    return pl.pallas_call(
        _router_kernel,
        out_shape=(jax.ShapeDtypeStruct((n * slab, 128), SLAB_DTYPE), jax.ShapeDtypeStruct((n, lanes), F32)),
        grid=(n // tm,),
        in_specs=[
            pl.BlockSpec((tm, k), lambda i: (i, 0)),
            pl.BlockSpec((1, k), lambda i: (0, 0)),
            pl.BlockSpec((k, lanes), lambda i: (0, 0)),
        ],
        out_specs=(pl.BlockSpec((tm * slab, 128), lambda i: (i, 0)), pl.BlockSpec((tm, lanes), lambda i: (i, 0))),
        compiler_params=_params(("parallel",), 32),
        name="moe_router",
    )(x, gain, w_router)


def _slab_copy(src_hbm, src_row, dst, dst_row, slab, sem):
    src_row = pl.multiple_of(src_row, slab)
    dst_row = pl.multiple_of(dst_row, slab)
    return pltpu.make_async_copy(src_hbm.at[pl.ds(src_row, slab)], dst.at[pl.ds(dst_row, slab)], sem)


def _start_slab_gather(src_hbm, idx_ref, base, dst, slab, sem):
    def body(r, carry):
        _slab_copy(src_hbm, idx_ref[base + r], dst, r * slab, slab, sem).start()
        return carry
    lax.fori_loop(0, dst.shape[0] // slab, body, 0, unroll=8)


def _start_slab_gather_part(src_hbm, idx_ref, base, dst, slab, sem, part, n_parts):
    per = dst.shape[0] // slab // n_parts
    for r in range(part * per, (part + 1) * per):
        _slab_copy(src_hbm, idx_ref[base + r], dst, r * slab, slab, sem).start()


def _wait_slab_gather(src_hbm, dst, slab, sem):
    def body(r, carry):
        _slab_copy(src_hbm, 0, dst, r * slab, slab, sem).wait()
        return carry
    lax.fori_loop(0, dst.shape[0] // slab, body, 0, unroll=8)


def _expert_weight_copies(wgu_hbm, wdn_hbm, layer, expert, wgu_f32, wdn_f32, wslot, wsem):
    return (pltpu.make_async_copy(wgu_hbm.at[layer, expert], wgu_f32.at[wslot], wsem.at[wslot, 0]),
            pltpu.make_async_copy(wdn_hbm.at[layer, expert], wdn_f32.at[wslot], wsem.at[wslot, 1]))


def _experts_kernel(te_ref, nu_ref, rt_ref, first_ref, wslot_ref, next_ref, x_hbm, wgu_hbm, wdn_hbm, o_ref,
                    xbuf, sem, wgu_f32, wdn_f32, wsem, wgu_bf, wdn_bf, *, layer):
    t = pl.program_id(0)
    slab = wgu_bf.shape[0] // SLAB_WIDTH
    tm = xbuf.shape[1] // slab
    n_used = nu_ref[0]
    slot = lax.rem(t, 2)
    weights = functools.partial(_expert_weight_copies, wgu_hbm, wdn_hbm, layer,
                                wgu_f32=wgu_f32, wdn_f32=wdn_f32, wsem=wsem)

    @pl.when((t == 0) & (n_used > 0))
    def _():
        _start_slab_gather(x_hbm, rt_ref, 0, xbuf.at[0], slab, sem.at[0])
        for c in weights(te_ref[0], wslot=0):
            c.start(priority=1)

    @pl.when((t < n_used) & (first_ref[t] == 1))
    def _():
        ws = wslot_ref[t]

        @pl.when(next_ref[t] >= 0)
        def _():
            for c in weights(next_ref[t], wslot=1 - ws):
                c.start(priority=1)

        for c in weights(te_ref[t], wslot=ws):
            c.wait()
        wgu_bf[...] = wgu_f32[ws].astype(BF16)
        wdn_bf[...] = wdn_f32[ws].astype(BF16)

    def tile(fetch_next):
        n_parts = 2 * MOE_COL_CHUNKS

        def start_next_tile_part(part):
            if fetch_next:
                _start_slab_gather_part(x_hbm, rt_ref, (t + 1) * tm, xbuf.at[1 - slot], slab, sem.at[1 - slot],
                                        part, n_parts)

        _wait_slab_gather(x_hbm, xbuf.at[slot], slab, sem.at[slot])
        x = _from_slabs(xbuf.at[slot], tm).astype(BF16)
        f2 = wgu_bf.shape[1]
        cw = f2 // MOE_COL_CHUNKS
        hg = []
        for c in range(MOE_COL_CHUNKS):
            start_next_tile_part(c)
            hg.append(jnp.dot(x, wgu_bf[:, c * cw:(c + 1) * cw], preferred_element_type=F32))
        hg = jnp.concatenate(hg, axis=1)
        a = hg[:, :f2 // 2]
        hact = ((a * _sigmoid(a)) * hg[:, f2 // 2:]).astype(BF16)
        dw = wdn_bf.shape[1] // MOE_COL_CHUNKS
        for c in range(MOE_COL_CHUNKS):
            start_next_tile_part(MOE_COL_CHUNKS + c)
            y = jnp.dot(hact, wdn_bf[:, c * dw:(c + 1) * dw], preferred_element_type=F32)
            _to_slabs(o_ref, y, first_row=c * dw // SLAB_WIDTH)

    pl.when(t + 1 < n_used)(functools.partial(tile, True))
    pl.when(t + 1 == n_used)(functools.partial(tile, False))

    @pl.when(t >= n_used)
    def _():
        o_ref[...] = jnp.zeros_like(o_ref)


def _experts(xn_slabs, plan, w_gu, w_dn, *, layer, tm):
    p = plan[2].shape[0]
    d, f2 = w_gu.shape[-2:]
    f = w_dn.shape[-2]
    slab = d // SLAB_WIDTH
    any_space = pl.BlockSpec(memory_space=pl.ANY)
    grid_spec = pltpu.PrefetchScalarGridSpec(
        num_scalar_prefetch=len(plan),
        grid=(p // tm,),
        in_specs=[any_space, any_space, any_space],
        out_specs=pl.BlockSpec((tm * slab, 128), lambda t, *_: (t, 0)),
        scratch_shapes=[pltpu.VMEM((2, tm * slab, 128), SLAB_DTYPE), pltpu.SemaphoreType.DMA((2,)),
                        pltpu.VMEM((2, d, f2), F32), pltpu.VMEM((2, f, d), F32), pltpu.SemaphoreType.DMA((2, 2)),
                        pltpu.VMEM((d, f2), BF16), pltpu.VMEM((f, d), BF16)],
    )
    return pl.pallas_call(
        functools.partial(_experts_kernel, layer=layer),
        out_shape=jax.ShapeDtypeStruct((p * slab, 128), SLAB_DTYPE),
        grid_spec=grid_spec,
        compiler_params=_params(("arbitrary",), 56),
        name="moe_experts",
    )(*plan, xn_slabs, w_gu, w_dn)


def _route(info, *, tm, slab):
    n = info.shape[0]
    e = info[:, :2].astype(jnp.int32).reshape(-1)
    onehot = (e[:, None] == jnp.arange(N_EXPERTS, dtype=jnp.int32)[None, :]).astype(jnp.int32)
    before = jnp.cumsum(onehot, axis=0) - onehot
    counts = jnp.sum(onehot, axis=0)
    padded = ((counts + tm - 1) // tm) * tm
    ends = jnp.cumsum(padded)
    starts = ends - padded
    pos = jnp.sum(onehot * (starts[None, :] + before), axis=1)
    p = ((2 * n + N_EXPERTS * (tm - 1)) // tm + 1) * tm
    row_start = jnp.zeros((p,), jnp.int32).at[pos].set((jnp.arange(2 * n, dtype=jnp.int32) // 2) * slab,
                                                       unique_indices=True, indices_are_sorted=False)
    tile_start = jnp.arange(p // tm, dtype=jnp.int32) * tm
    n_used = (ends[-1] // tm).astype(jnp.int32)
    used = tile_start < ends[-1]
    te = jnp.sum((tile_start[:, None] >= ends[None, :]).astype(jnp.int32), axis=1)
    last = jnp.sum((jnp.maximum(ends[-1] - 1, 0) >= ends).astype(jnp.int32))
    te = jnp.where(used, te, last).astype(jnp.int32)
    first = used & (te != jnp.concatenate([jnp.full((1,), -1, jnp.int32), te[:-1]]))
    wslot = lax.rem(jnp.cumsum(first.astype(jnp.int32)) - 1, 2)
    later = used[None, :] & (te[None, :] > te[:, None])
    nxt = jnp.min(jnp.where(later, te[None, :], N_EXPERTS), axis=1)
    nxt = jnp.where(nxt < N_EXPERTS, nxt, -1)
    plan = (te, n_used.reshape(1), row_start, first.astype(jnp.int32), wslot.astype(jnp.int32),
            nxt.astype(jnp.int32))
    pos2 = (pos * slab).reshape(n, 2)
    return plan, pos2[:, 0], pos2[:, 1]


def _ple_kernel(pos0_ref, pos1_ref, h_ref, info_ref, y_hbm, p_ref, gp_ref, wg_ref, wp_ref, *rest, final):
    if final:
        gf_ref, o_ref, ybuf, sem = rest
    else:
        o_ref, ybuf, sem = rest
    i = pl.program_id(0)
    tm, d = h_ref.shape
    slab = d // SLAB_WIDTH
    slot = lax.rem(i, 2)

    @pl.when(i == 0)
    def _():
        _start_slab_gather(y_hbm, pos0_ref, 0, ybuf.at[0, 0], slab, sem.at[0])
        _start_slab_gather(y_hbm, pos1_ref, 0, ybuf.at[0, 1], slab, sem.at[0])

    def tile(fetch_next):
        def start_next_tile_part(part):
            if fetch_next:
                for which, pos_ref in enumerate((pos0_ref, pos1_ref)):
                    _start_slab_gather_part(y_hbm, pos_ref, (i + 1) * tm, ybuf.at[1 - slot, which], slab,
                                            sem.at[1 - slot], part, MOE_COL_CHUNKS)

        _wait_slab_gather(y_hbm, ybuf.at[slot, 0], slab, sem.at[slot])
        _wait_slab_gather(y_hbm, ybuf.at[slot, 1], slab, sem.at[slot])
        info = info_ref[...]
        y0 = _from_slabs(ybuf.at[slot, 0], tm)
        y1 = _from_slabs(ybuf.at[slot, 1], tm)
        h = h_ref[...] + (info[:, 2:3] * y0 + info[:, 3:4] * y1)
        hn = _rms(h, gp_ref[...]).astype(BF16)
        pb = p_ref[...].astype(BF16)
        cw = d // MOE_COL_CHUNKS
        out = []
        for c in range(MOE_COL_CHUNKS):
            start_next_tile_part(c)
            cols = slice(c * cw, (c + 1) * cw)
            gate = _sigmoid(jnp.dot(hn, wg_ref[:, cols], preferred_element_type=F32))
            proj = jnp.dot(pb, wp_ref[:, cols], preferred_element_type=F32)
            out.append(h[:, cols] + gate * proj)
        out = jnp.concatenate(out, axis=1)
        if final:
            out = _rms(out, gf_ref[...])
        o_ref[...] = out

    last = pl.num_programs(0) - 1
    pl.when(i < last)(functools.partial(tile, True))
    pl.when(i == last)(functools.partial(tile, False))


def _ple(h, info, y_sorted, pos0, pos1, *, p_emb, g_ple, w_gate, w_proj, g_final, tm, row0=0, n_rows=None):
    d = h.shape[1]
    n = h.shape[0] if n_rows is None else n_rows
    pd = p_emb.shape[1]
    r0 = row0 // tm
    row = lambda i, p0, p1: (r0 + i, 0)
    fix = lambda i, p0, p1: (0, 0)
    in_specs = [
        pl.BlockSpec((tm, d), row), pl.BlockSpec((tm, info.shape[1]), row), pl.BlockSpec(memory_space=pl.ANY),
        pl.BlockSpec((tm, pd), row), pl.BlockSpec((1, d), fix),
        pl.BlockSpec((d, d), fix), pl.BlockSpec((pd, d), fix),
    ]
    args = [h, info, y_sorted, p_emb, g_ple, w_gate, w_proj]
    if g_final is not None:
        in_specs.append(pl.BlockSpec((1, d), fix))
        args.append(g_final)
    grid_spec = pltpu.PrefetchScalarGridSpec(
        num_scalar_prefetch=2,
        grid=(n // tm,),
        in_specs=in_specs,
        out_specs=pl.BlockSpec((tm, d), lambda i, p0, p1: (i, 0)),
        scratch_shapes=[pltpu.VMEM((2, 2, tm * (d // SLAB_WIDTH), 128), SLAB_DTYPE), pltpu.SemaphoreType.DMA((2,))],
    )
    return pl.pallas_call(
        functools.partial(_ple_kernel, final=g_final is not None),
        out_shape=jax.ShapeDtypeStruct((n, d), F32),
        grid_spec=grid_spec,
        compiler_params=_params(("arbitrary",), 52),
        name="ple",
    )(pos0, pos1, *args)


def _linear_res_kernel(ap_ref, as_ref, w_ref, h_ref, o_ref, *, n_first):
    a = _pick_group(ap_ref, as_ref, n_first).astype(BF16)
    o_ref[...] = h_ref[...] + jnp.dot(a, w_ref[...], preferred_element_type=F32)


def _linear_res(a_p, a_s, w, h, *, tm):
    n, d = h.shape
    k = a_p.shape[1]
    n_first = a_p.shape[0]
    return pl.pallas_call(
        functools.partial(_linear_res_kernel, n_first=n_first),
        out_shape=jax.ShapeDtypeStruct((n, d), F32),
        grid=(n // tm,),
        in_specs=[*_two_group_specs(tm, k, n_first), pl.BlockSpec((k, d), lambda i: (0, 0)),
                  pl.BlockSpec((tm, d), lambda i: (i, 0))],
        out_specs=pl.BlockSpec((tm, d), lambda i: (i, 0)),
        compiler_params=_params(("parallel",), 40),
        name="attn_out",
    )(a_p, a_s, w, h)


def _band_attn_kernel(q0_ref, q1_ref, q2_ref, k_ref, v_ref, o_ref, og_ref, lse_ref, *, seq):
    q_refs = (q0_ref, q1_ref, q2_ref)
    row = lax.broadcasted_iota(jnp.int32, (BAND, BAND), 0)
    col = lax.broadcasted_iota(jnp.int32, (BAND, BAND), 1)
    cur_ok = col <= row
    prev_ok = col >= row
    row2 = lax.broadcasted_iota(jnp.int32, (BAND, 2 * BAND), 0)
    col2 = lax.broadcasted_iota(jnp.int32, (BAND, 2 * BAND), 1)
    both_ok = ((col2 < BAND) & (col2 >= row2)) | ((col2 >= BAND) & (col2 - BAND <= row2))
    del prev_ok
    nt = (((1,), (1,)), ((), ()))

    def rows(start, dil):
        return pl.ds(start, BAND) if dil == 1 else pl.ds(start, BAND, stride=dil)

    for gi, (win, dil) in enumerate(DIL_GROUPS):
        assert win // dil == BAND
        sub_len = seq // dil
        for r in range(dil):
            kp = vp = None
            for n in range(sub_len // BAND):
                sl = rows(r + dil * BAND * n, dil)
                qb = q_refs[gi][sl, :].astype(BF16)
                kc = k_ref[sl, :].astype(BF16)
                vc = v_ref[sl, :].astype(BF16)
                if kp is None:
                    keys, vals, ok = kc, vc, cur_ok
                else:
                    keys = jnp.concatenate([kp, kc], axis=0)
                    vals = jnp.concatenate([vp, vc], axis=0)
                    ok = both_ok
                s = jnp.where(ok, lax.dot_general(qb, keys, nt, preferred_element_type=F32), NEG_BIG)
                m = jnp.max(s, axis=-1, keepdims=True)
                p = jnp.exp(s - m)
                l = jnp.sum(p, axis=-1, keepdims=True)
                acc = jnp.dot(p.astype(BF16), vals, preferred_element_type=F32)
                og_ref[gi, sl, :] = acc / l
                lse_ref[gi, sl, :] = jnp.broadcast_to(m + jnp.log(l), (BAND, HEAD_DIM))
                kp, vp = kc, vc

    l0, l1, l2 = lse_ref[0], lse_ref[1], lse_ref[2]
    mx = jnp.maximum(jnp.maximum(l0, l1), l2)
    w0, w1, w2 = jnp.exp(l0 - mx), jnp.exp(l1 - mx), jnp.exp(l2 - mx)
    o_ref[...] = (w0 * og_ref[0] + w1 * og_ref[1] + w2 * og_ref[2]) / (w0 + w1 + w2)


def _band_attention(q, k, v, *, batch, seq, n_heads):
    blk = (seq, HEAD_DIM)
    qspec = lambda gi: pl.BlockSpec(blk, lambda b, h: (b, gi * n_heads + h))
    kv = pl.BlockSpec(blk, lambda b, h: (b, h))
    return pl.pallas_call(
        functools.partial(_band_attn_kernel, seq=seq),
        out_shape=jax.ShapeDtypeStruct((batch * seq, n_heads * HEAD_DIM), F32),
        grid=(batch, n_heads),
        in_specs=[qspec(0), qspec(1), qspec(2), kv, kv],
        out_specs=kv,
        scratch_shapes=[pltpu.VMEM((3, seq, HEAD_DIM), F32), pltpu.VMEM((3, seq, HEAD_DIM), F32)],
        compiler_params=_params(("parallel", "parallel"), 40),
        name="band_attention",
    )(q, q, q, k, v)


def _decode_rows(w_buf, dec_seq):
    max_dil = max(d for _, d in DIL_GROUPS)
    dense_from = w_buf
    for win, dil in DIL_GROUPS:
        if dil < max_dil:
            dense_from = min(dense_from, w_buf - win)
    dense_from = max((dense_from // max_dil) * max_dil, 0)
    return max_dil, dense_from


def _decode_bias(w_buf, dec_seq, key_pos, n_pad):
    bias = np.full((len(DIL_GROUPS) * dec_seq, n_pad), NEG_BIG, np.float32)
    for gi, (win, dil) in enumerate(DIL_GROUPS):
        for j in range(dec_seq):
            dist = (w_buf + j) - key_pos
            ok = (dist >= 0) & (dist <= win) & (dist % dil == 0)
            bias[gi * dec_seq + j, :len(key_pos)][ok] = 0.0
    return bias


def _decode_attn_kernel(q_ref, kn_ref, vn_ref, ka_ref, kb_ref, va_ref, vb_ref, bias_ref, o_ref, *,
                        n_heads, dec_seq, n_groups, n_pad):
    bias = bias_ref[...]
    nt = (((1,), (1,)), ((), ()))
    na = ka_ref.shape[0] * (ka_ref.shape[1] // n_heads)
    nb = kb_ref.shape[0] * (kb_ref.shape[1] // n_heads)
    tail = n_pad - na - nb - dec_seq

    def head_rows(a_ref, b_ref, n_ref, h):
        ra = a_ref.shape[1] // n_heads
        rb = b_ref.shape[1] // n_heads
        xa = a_ref[:, pl.ds(h, ra, stride=n_heads), :].reshape(na, HEAD_DIM)
        xb = b_ref[:, pl.ds(h, rb, stride=n_heads), :].reshape(nb, HEAD_DIM)
        xn = n_ref[:, h * HEAD_DIM:(h + 1) * HEAD_DIM]
        parts = [xa, xb, xn]
        if tail:
            parts.append(jnp.zeros((tail, HEAD_DIM), F32))
        return jnp.concatenate(parts, axis=0).astype(BF16)

    outs = []
    for h in range(n_heads):
        kh = head_rows(ka_ref, kb_ref, kn_ref, h)
        vh = head_rows(va_ref, vb_ref, vn_ref, h)
        qh = jnp.concatenate(
            [q_ref[:, (gi * n_heads + h) * HEAD_DIM:(gi * n_heads + h + 1) * HEAD_DIM] for gi in range(n_groups)],
            axis=0).astype(BF16)
        s = lax.dot_general(qh, kh, nt, preferred_element_type=F32) + bias
        m = jnp.max(s, axis=-1, keepdims=True)
        p = jnp.exp(s - m)
        l = jnp.sum(p, axis=-1, keepdims=True)
        og = jnp.dot(p.astype(BF16), vh, preferred_element_type=F32) / l
        lse = m + jnp.log(l)
        ls = [lse[gi * dec_seq:(gi + 1) * dec_seq] for gi in range(n_groups)]
        mx = functools.reduce(jnp.maximum, ls)
        ws = [jnp.exp(x - mx) for x in ls]
        num = sum(w * og[gi * dec_seq:(gi + 1) * dec_seq] for gi, w in enumerate(ws))
        outs.append(num / sum(ws))
    o_ref[...] = jnp.concatenate(outs, axis=1)


def _decode_attention(q, k_new, v_new, cache_k, cache_v, *, q_row0, n_heads):
    batch, w_buf = cache_k.shape[:2]
    dec_seq = k_new.shape[0] // batch
    n_groups = len(DIL_GROUPS)
    comb, dense_from = _decode_rows(w_buf, dec_seq)
    assert dec_seq <= comb and comb % dec_seq == 0 and dec_seq % 8 == 0 and w_buf % comb == 0
    n_comb = dense_from // comb
    n_dense = (w_buf - dense_from) // comb
    key_pos = np.concatenate([
        (np.arange(n_comb)[:, None] * comb + np.arange(dec_seq)[None, :]).reshape(-1),
        dense_from + np.arange(w_buf - dense_from),
        w_buf + np.arange(dec_seq)])
    n_pad = -(-len(key_pos) // 128) * 128
    bias = jnp.asarray(_decode_bias(w_buf, dec_seq, key_pos, n_pad))
    ck = cache_k.reshape(batch, w_buf // comb, comb * n_heads, HEAD_DIM)
    cv = cache_v.reshape(batch, w_buf // comb, comb * n_heads, HEAD_DIM)
    tok = lambda width, row0=0: pl.BlockSpec((dec_seq, width), lambda b: (row0 // dec_seq + b, 0))
    comb_spec = pl.BlockSpec((None, n_comb, dec_seq * n_heads, HEAD_DIM), lambda b: (b, 0, 0, 0))
    dense_spec = pl.BlockSpec((None, n_dense, comb * n_heads, HEAD_DIM), lambda b: (b, n_comb // n_dense, 0, 0))
    assert n_comb % n_dense == 0
    return pl.pallas_call(
        functools.partial(_decode_attn_kernel, n_heads=n_heads, dec_seq=dec_seq, n_groups=n_groups, n_pad=n_pad),
        out_shape=jax.ShapeDtypeStruct((batch * dec_seq, n_heads * HEAD_DIM), F32),
        grid=(batch,),
        in_specs=[tok(q.shape[1], q_row0), tok(k_new.shape[1]), tok(v_new.shape[1]),
                  comb_spec, dense_spec, comb_spec, dense_spec,
                  pl.BlockSpec(bias.shape, lambda b: (0, 0))],
        out_specs=pl.BlockSpec((dec_seq, n_heads * HEAD_DIM), lambda b: (b, 0)),
        compiler_params=_params(("parallel",), 48),
        name="decode_attention",
    )(q, k_new, v_new, ck, ck, cv, cv, bias)


def _moe_and_ple(h, p_emb, layer, w, g_final, *, tm_moe, split=None):
    n, d = h.shape
    lanes = 128
    w_router = jnp.zeros((d, lanes), F32)
    w_router = w_router.at[:, :N_EXPERT_GROUPS].set(w["moe_w_rg"][layer])
    w_router = w_router.at[:, N_EXPERT_GROUPS:N_EXPERT_GROUPS + N_EXPERTS].set(w["moe_w_re"][layer])
    xn, info = _router(h, w["norm_ffn"][layer][None, :], w_router, tm=512)
    plan, pos0, pos1 = _route(info, tm=tm_moe, slab=d // SLAB_WIDTH)
    y_sorted = _experts(xn, plan, w["moe_w_gu"], w["moe_w_dn"], layer=layer, tm=tm_moe)
    ple = functools.partial(_ple, h, info, y_sorted, p_emb=p_emb, g_ple=w["norm_ple"][layer][None, :],
                            w_gate=w["ple_w_gate"][layer].astype(BF16), w_proj=w["ple_w_proj"][layer].astype(BF16),
                            g_final=g_final, tm=256)
    if split is None:
        return ple(pos0, pos1)
    return (ple(pos0[:split], pos1[:split], row0=0, n_rows=split),
            ple(pos0[split:], pos1[split:], row0=split, n_rows=n - split))


def kernel(x_prompt, x_sample, p_prompt, p_sample, state_gla, cache_k, cache_v, norm_mix, norm_ffn, norm_ple,
           norm_kv, norm_final, gla_w_in, gla_w_a2, gla_b_a, gla_g_out, gla_w_out, w_kv, dil_w_q, dil_w_out,
           moe_w_rg, moe_w_re, moe_w_gu, moe_w_dn, ple_w_gate, ple_w_proj):
    w = dict(norm_ffn=norm_ffn, norm_ple=norm_ple, moe_w_rg=moe_w_rg, moe_w_re=moe_w_re, moe_w_gu=moe_w_gu,
             moe_w_dn=moe_w_dn, ple_w_gate=ple_w_gate, ple_w_proj=ple_w_proj)
    bp, tp, d = x_prompt.shape
    bs, ts, _ = x_sample.shape
    n_p, n_s = bp * tp, bs * ts
    depth = p_prompt.shape[0]
    assert depth == 2 and state_gla.shape[0] == 1
    heads, dk, dv = state_gla.shape[2:]
    n_kv = cache_k.shape[2]
    past_len = cache_k.shape[1]
    qk_w, v_w = heads * dk, heads * dv
    lowrank = gla_w_a2.shape[1]

    x = jnp.concatenate([x_prompt.reshape(n_p, d), x_sample.reshape(n_s, d)], axis=0)
    p_emb = jnp.concatenate([p_prompt.reshape(depth, n_p, -1), p_sample.reshape(depth, n_s, -1)], axis=1)

    half = HEAD_DIM // 2
    inv = ROPE_THETA ** (-jnp.arange(half, dtype=F32) / half)
    pos = jnp.concatenate([jnp.tile(jnp.arange(tp), bp), jnp.tile(past_len + jnp.arange(ts), bs)]).astype(F32)
    ang = pos[:, None] * inv[None, :]
    rope = (jnp.concatenate([jnp.cos(ang), jnp.cos(ang)], axis=1),
            jnp.concatenate([-jnp.sin(ang), jnp.sin(ang)], axis=1))

    g_mix0 = norm_mix[0][None, :]
    w_in = gla_w_in[0].astype(BF16)
    main_w = 2 * qk_w + 2 * v_w
    tm_wide = 1024 if (n_p + n_s) % 1024 == 0 else 512
    z = _norm_linear(x, g_mix0, w_in, col_start=0, n_cols=main_w, tn=1024, tm=tm_wide)
    w_a1 = jnp.zeros((d, 128), BF16).at[:, :lowrank].set(w_in[:, main_w:])
    w_a2 = jnp.zeros((128, qk_w), F32).at[:lowrank].set(gla_w_a2[0])
    g = _gla_gate(x, g_mix0, w_a1, w_a2, gla_b_a[0][None, :], tm=512)
    c_p = int(np.gcd(tp, GLA_CHUNK))
    c_s = int(np.gcd(ts, GLA_CHUNK))
    gla = functools.partial(_gla_scan, z, g, heads=heads, dk=dk, dv=dv)
    o_p, sg_p = gla(None, row0=0, batch=bp, seq=tp, c=c_p)
    o_s, sg_s = gla(state_gla[0], row0=n_p, batch=bs, seq=ts, c=c_s)
    h = _gla_out(o_p, o_s, z, x, gla_g_out[0][None, :], gla_w_out[0].astype(BF16), heads=heads, tm=256)
    h = _moe_and_ple(h, p_emb[0], 0, w, None, tm_moe=256)

    w_kv_b = w_kv.astype(BF16)
    kv_w = n_kv * HEAD_DIM
    g_kv = norm_kv[None, :]
    kv_lin = functools.partial(_norm_linear, h, g_kv, w_kv_b, n_cols=kv_w, tn=kv_w, tm=512)
    k_p = kv_lin(col_start=0, rope=rope, row0=0, n_rows=n_p)
    k_s = kv_lin(col_start=0, rope=rope, row0=n_p, n_rows=n_s)
    v_p = kv_lin(col_start=kv_w, row0=0, n_rows=n_p)
    v_s = kv_lin(col_start=kv_w, row0=n_p, n_rows=n_s)
    q_all = _norm_linear(h, norm_mix[1][None, :], dil_w_q[0].astype(BF16), col_start=0,
                         n_cols=dil_w_q.shape[2], tn=kv_w, tm=tm_wide, rope=rope, scale=HEAD_DIM ** -0.5)

    a_p = _band_attention(q_all, k_p, v_p, batch=bp, seq=tp, n_heads=n_kv)
    a_s = _decode_attention(q_all, k_s, v_s, cache_k, cache_v, q_row0=n_p, n_heads=n_kv)
    h = _linear_res(a_p, a_s, dil_w_out[0].astype(BF16), h, tm=512)
    y_p, y_s = _moe_and_ple(h, p_emb[1], 1, w, norm_final[None, :], tm_moe=256, split=n_p)

    rows = min(past_len, tp)
    as_heads = lambda a, b, t: a.reshape(b, t, n_kv, HEAD_DIM)
    return (y_p.reshape(bp, tp, d), y_s.reshape(bs, ts, d), sg_p[None], sg_s[None],
            as_heads(k_p, bp, tp)[:, tp - rows:], as_heads(v_p, bp, tp)[:, tp - rows:],
            as_heads(k_s, bs, ts), as_heads(v_s, bs, ts))
```

```python
import functools

import numpy as np
import jax
import jax.numpy as jnp
from jax import lax
from jax.experimental import pallas as pl
from jax.experimental.pallas import tpu as pltpu

F32 = jnp.float32
BF16 = jnp.bfloat16
HIGHEST = lax.Precision.HIGHEST

NORM_EPS = 1e-6
GLA_TAU = 16.0
GLA_CHUNK = 64
GLA_PAD = 128
HEAD_DIM = 128
BAND = 128
DIL_GROUPS = ((128, 1), (512, 4), (2048, 16))
ROPE_THETA = 10000.0
N_EXPERT_GROUPS = 4
EXPERTS_PER_GROUP = 8
N_EXPERTS = N_EXPERT_GROUPS * EXPERTS_PER_GROUP
NEG_BIG = -1e30
MIB = 1024 * 1024


def _params(semantics, vmem_mib):
    return pltpu.CompilerParams(dimension_semantics=semantics, vmem_limit_bytes=vmem_mib * MIB)


def _rms(x, gain):
    var = jnp.mean(x * x, axis=-1, keepdims=True)
    return x * lax.rsqrt(var + NORM_EPS) * gain


def _sigmoid(x):
    return 1.0 / (1.0 + jnp.exp(-x))


def _norm_linear_kernel(x_ref, g_ref, w_ref, *rest, rope, scale):
    if rope:
        cos_ref, sin_ref, o_ref, xn_ref = rest
    else:
        o_ref, xn_ref = rest

    @pl.when(pl.program_id(1) == 0)
    def _():
        xn_ref[...] = _rms(x_ref[...], g_ref[...]).astype(BF16)

    acc = jnp.dot(xn_ref[...], w_ref[...], preferred_element_type=F32)
    if rope:
        cos = cos_ref[...]
        sin = sin_ref[...]
        parts = []
        for c in range(acc.shape[1] // HEAD_DIM):
            y = acc[:, c * HEAD_DIM:(c + 1) * HEAD_DIM]
            parts.append(y * cos + pltpu.roll(y, HEAD_DIM // 2, 1) * sin)
        acc = parts[0] if len(parts) == 1 else jnp.concatenate(parts, axis=1)
    if scale != 1.0:
        acc = acc * scale
    o_ref[...] = acc


def _norm_linear(x, gain, w, *, col_start, n_cols, tn, tm, rope=None, scale=1.0, row0=0, n_rows=None):
    k = x.shape[1]
    n = x.shape[0] if n_rows is None else n_rows
    grid = (n // tm, n_cols // tn)
    c0 = col_start // tn
    r0 = row0 // tm
    in_specs = [
        pl.BlockSpec((tm, k), lambda i, j: (r0 + i, 0)),
        pl.BlockSpec((1, k), lambda i, j: (0, 0)),
        pl.BlockSpec((k, tn), lambda i, j: (0, c0 + j)),
    ]
    args = [x, gain, w]
    if rope is not None:
        in_specs += [pl.BlockSpec((tm, HEAD_DIM), lambda i, j: (r0 + i, 0))] * 2
        args += list(rope)
    return pl.pallas_call(
        functools.partial(_norm_linear_kernel, rope=rope is not None, scale=scale),
        out_shape=jax.ShapeDtypeStruct((n, n_cols), F32),
        grid=grid,
        in_specs=in_specs,
        out_specs=pl.BlockSpec((tm, tn), lambda i, j: (i, j)),
        scratch_shapes=[pltpu.VMEM((tm, k), BF16)],
        compiler_params=_params(("parallel", "arbitrary"), 48),
        name="norm_linear",
    )(*args)


def _gla_gate_kernel(x_ref, gn_ref, w1_ref, w2_ref, b_ref, o_ref):
    xn = _rms(x_ref[...], gn_ref[...]).astype(BF16)
    a1 = jnp.dot(xn, w1_ref[...], preferred_element_type=F32)
    pre = jnp.dot(a1, w2_ref[...], precision=HIGHEST, preferred_element_type=F32) + b_ref[...]
    o_ref[...] = (jnp.minimum(pre, 0.0) - jnp.log1p(jnp.exp(-jnp.abs(pre)))) * (1.0 / GLA_TAU)


def _gla_gate(x, gain, w1, w2, b_a, *, tm):
    n, k = x.shape
    r, qk = w2.shape
    return pl.pallas_call(
        _gla_gate_kernel,
        out_shape=jax.ShapeDtypeStruct((n, qk), F32),
        grid=(n // tm,),
        in_specs=[
            pl.BlockSpec((tm, k), lambda i: (i, 0)),
            pl.BlockSpec((1, k), lambda i: (0, 0)),
            pl.BlockSpec((k, r), lambda i: (0, 0)),
            pl.BlockSpec((r, qk), lambda i: (0, 0)),
            pl.BlockSpec((1, qk), lambda i: (0, 0)),
        ],
        out_specs=pl.BlockSpec((tm, qk), lambda i: (i, 0)),
        compiler_params=_params(("parallel",), 32),
        name="gla_gate",
    )(x, gain, w1, w2, b_a)


def _gla_kernel(*refs, c, heads, lanes, has_s0, q_scale):
    lane_refs = [refs[4 * i:4 * i + 4] for i in range(lanes)]
    s0_ref = refs[4 * lanes] if has_s0 else None
    o_ref, so_ref, s_scr = refs[-3:]
    n = pl.program_id(1)
    dk = lane_refs[0][0].shape[1] // heads
    dv = lane_refs[0][2].shape[1] // heads

    @pl.when(n == 0)
    def _():
        if has_s0:
            s_scr[...] = s0_ref[...]
        else:
            s_scr[...] = jnp.zeros_like(s_scr)

    def pad(a):
        if c == GLA_PAD:
            return a
        return jnp.concatenate([a, jnp.zeros((GLA_PAD - c, a.shape[1]), a.dtype)], axis=0)

    row = lax.broadcasted_iota(jnp.int32, (GLA_PAD, GLA_PAD), 0)
    col = lax.broadcasted_iota(jnp.int32, (GLA_PAD, GLA_PAD), 1)
    tri = jnp.where(row >= col, 1.0, 0.0).astype(F32)
    rc = lax.broadcasted_iota(jnp.int32, (c, c), 0)
    cc = lax.broadcasted_iota(jnp.int32, (c, c), 1)
    for lane, (q_ref, k_ref, v_ref, g_ref) in enumerate(lane_refs):
        bp = jnp.dot(tri, pad(g_ref[...]), precision=HIGHEST, preferred_element_type=F32)
        b = bp[:c]
        k = k_ref[...]
        v = v_ref[...]
        qe = (q_ref[...] * q_scale * jnp.exp(b)).astype(BF16)
        ke = (k * jnp.exp(-b)).astype(BF16)
        vb = v.astype(BF16)
        k_t = pad(k).T
        b_t = bp.T
        b_last = b_t[:, c - 1:c]
        kd_t = (k_t * jnp.exp(b_last - b_t)).astype(BF16)
        decay = jnp.exp(b_last)
        vp = pad(v).astype(BF16)
        for h in range(heads):
            ks = slice(h * dk, (h + 1) * dk)
            vs = slice(h * dv, (h + 1) * dv)
            s_old = s_scr[lane, h]
            o = jnp.dot(qe[:, ks], s_old.astype(BF16), preferred_element_type=F32)
            a = lax.dot_general(qe[:, ks], ke[:, ks], (((1,), (1,)), ((), ())), preferred_element_type=F32)
            a = jnp.where(rc >= cc, a, 0.0)
            o_ref[lane, :, vs] = o + jnp.dot(a.astype(BF16), vb[:, vs], preferred_element_type=F32)
            s_scr[lane, h] = decay[ks] * s_old + jnp.dot(kd_t[ks], vp[:, vs], preferred_element_type=F32)

    @pl.when(n == pl.num_programs(1) - 1)
    def _():
        so_ref[...] = s_scr[...]


def _gla_scan(z, g, s0, *, row0, batch, seq, heads, dk, dv, c, lanes):
    nchunk = seq // c
    rb0 = row0 // c
    qk_w, v_w = heads * dk, heads * dv
    in_specs, args = [], []
    for lane in range(lanes):
        rows = lambda b, n, lane=lane: rb0 + (b * lanes + lane) * nchunk + n
        in_specs += [
            pl.BlockSpec((c, qk_w), lambda b, n, rows=rows: (rows(b, n), 0)),
            pl.BlockSpec((c, qk_w), lambda b, n, rows=rows: (rows(b, n), 1)),
            pl.BlockSpec((c, v_w), lambda b, n, rows=rows: (rows(b, n), 2 * qk_w // v_w)),
            pl.BlockSpec((c, qk_w), lambda b, n, rows=rows: (rows(b, n), 0)),
        ]
        args += [z, z, z, g]
    state_spec = pl.BlockSpec((lanes, heads, dk, dv), lambda b, n: (b, 0, 0, 0))
    if s0 is not None:
        in_specs.append(state_spec)
        args.append(s0)
    o, s_out = pl.pallas_call(
        functools.partial(_gla_kernel, c=c, heads=heads, lanes=lanes, has_s0=s0 is not None,
                          q_scale=float(dk) ** -0.5),
        out_shape=(jax.ShapeDtypeStruct((batch, seq, v_w), F32),
                   jax.ShapeDtypeStruct((batch, heads, dk, dv), F32)),
        grid=(batch // lanes, nchunk),
        in_specs=in_specs,
        out_specs=(pl.BlockSpec((lanes, c, v_w), lambda b, n: (b, n, 0)), state_spec),
        scratch_shapes=[pltpu.VMEM((lanes, heads, dk, dv), F32)],
        compiler_params=_params(("parallel", "arbitrary"), 48),
        name="gla_scan",
    )(*args)
    return o.reshape(batch * seq, v_w), s_out


def _two_group_specs(tm, width, n_first):
    t_first = n_first // tm
    return (pl.BlockSpec((tm, width), lambda i: (jnp.minimum(i, t_first - 1), 0)),
            pl.BlockSpec((tm, width), lambda i: (jnp.maximum(i - t_first, 0), 0)))


def _pick_group(first_ref, second_ref, n_first):
    in_first = pl.program_id(0) < n_first // first_ref.shape[0]
    return jnp.where(in_first, first_ref[...], second_ref[...])


def _gla_out_kernel(op_ref, os_ref, r_ref, x_ref, go_ref, w_ref, h_ref, *, heads, n_first):
    o = _pick_group(op_ref, os_ref, n_first)
    dv = o.shape[1] // heads
    go = go_ref[...]
    parts = [_rms(o[:, h * dv:(h + 1) * dv], go) for h in range(heads)]
    on = jnp.concatenate(parts, axis=1)
    r = r_ref[...]
    y = (on * (r * _sigmoid(r))).astype(BF16)
    h_ref[...] = x_ref[...] + jnp.dot(y, w_ref[...], preferred_element_type=F32)


def _gla_out(o_p, o_s, z, x, g_out, w_out, *, heads, tm):
    n, d = x.shape
    vw = o_p.shape[1]
    n_first = o_p.shape[0]
    rblk = (z.shape[1] - vw) // vw
    return pl.pallas_call(
        functools.partial(_gla_out_kernel, heads=heads, n_first=n_first),
        out_shape=jax.ShapeDtypeStruct((n, d), F32),
        grid=(n // tm,),
        in_specs=[
            *_two_group_specs(tm, vw, n_first),
            pl.BlockSpec((tm, vw), lambda i: (i, rblk)),
            pl.BlockSpec((tm, d), lambda i: (i, 0)),
            pl.BlockSpec((1, vw // heads), lambda i: (0, 0)),
            pl.BlockSpec((vw, d), lambda i: (0, 0)),
        ],
        out_specs=pl.BlockSpec((tm, d), lambda i: (i, 0)),
        compiler_params=_params(("parallel",), 44),
        name="gla_out",
    )(o_p, o_s, z, x, g_out, w_out)


SLAB_WIDTH = 128
SLAB_DTYPE = F32


def _from_slabs(ref, rows):
    slab = ref.shape[0] // rows
    return jnp.concatenate([ref[pl.ds(s, rows, stride=slab), :] for s in range(slab)], axis=1)


def _to_slabs(ref, val):
    rows = val.shape[0]
    slab = ref.shape[0] // rows
    for s in range(slab):
        ref[pl.ds(s, rows, stride=slab), :] = val[:, s * 128:(s + 1) * 128]


def _router_kernel(x_ref, g_ref, w_ref, xn_ref, info_ref):
    xn = _rms(x_ref[...], g_ref[...])
    _to_slabs(xn_ref, xn)
    lg = jnp.dot(xn, w_ref[...], precision=HIGHEST, preferred_element_type=F32)
    lane = lax.broadcasted_iota(jnp.int32, lg.shape, 1).astype(F32)
    far = float(lg.shape[1])
    is_grp = lane < N_EXPERT_GROUPS
    lgm = jnp.where(is_grp, lg, NEG_BIG)
    gmax = jnp.max(lgm, axis=-1, keepdims=True)
    gsum = jnp.sum(jnp.where(is_grp, jnp.exp(lg - gmax), 0.0), axis=-1, keepdims=True)
    p_top = 1.0 / gsum
    g_top = jnp.min(jnp.where(lgm == gmax, lane, far), axis=-1, keepdims=True)
    lo = N_EXPERT_GROUPS + g_top * EXPERTS_PER_GROUP
    in_grp = (lane >= lo) & (lane < lo + EXPERTS_PER_GROUP)
    le = jnp.where(in_grp, lg, NEG_BIG)
    m0 = jnp.max(le, axis=-1, keepdims=True)
    i0 = jnp.min(jnp.where(le == m0, lane, far), axis=-1, keepdims=True)
    le1 = jnp.where(lane == i0, NEG_BIG, le)
    m1 = jnp.max(le1, axis=-1, keepdims=True)
    i1 = jnp.min(jnp.where(le1 == m1, lane, far), axis=-1, keepdims=True)
    t = jnp.exp(m1 - m0)
    w0 = p_top / (1.0 + t)
    w1 = p_top * t / (1.0 + t)
    info = jnp.where(lane == 0.0, i0 - N_EXPERT_GROUPS,
                     jnp.where(lane == 1.0, i1 - N_EXPERT_GROUPS,
                               jnp.where(lane == 2.0, w0, jnp.where(lane == 3.0, w1, 0.0))))
    info_ref[...] = info


def _router(x, gain, w_router, *, tm):
    n, k = x.shape
    lanes = w_router.shape[1]
    slab = k // SLAB_WIDTH
    return pl.pallas_call(
        _router_kernel,
        out_shape=(jax.ShapeDtypeStruct((n * slab, 128), SLAB_DTYPE), jax.ShapeDtypeStruct((n, lanes), F32)),
        grid=(n // tm,),
        in_specs=[
            pl.BlockSpec((tm, k), lambda i: (i, 0)),
            pl.BlockSpec((1, k), lambda i: (0, 0)),
            pl.BlockSpec((k, lanes), lambda i: (0, 0)),
        ],
        out_specs=(pl.BlockSpec((tm * slab, 128), lambda i: (i, 0)), pl.BlockSpec((tm, lanes), lambda i: (i, 0))),
        compiler_params=_params(("parallel",), 32),
        name="moe_router",
    )(x, gain, w_router)


def _slab_copy(src_hbm, src_row, dst, dst_row, slab, sem):
    src_row = pl.multiple_of(src_row, slab)
    dst_row = pl.multiple_of(dst_row, slab)
    return pltpu.make_async_copy(src_hbm.at[pl.ds(src_row, slab)], dst.at[pl.ds(dst_row, slab)], sem)


def _start_slab_gather(src_hbm, idx_ref, base, dst, slab, sem):
    def body(r, carry):
        _slab_copy(src_hbm, idx_ref[base + r], dst, r * slab, slab, sem).start()
        return carry
    lax.fori_loop(0, dst.shape[0] // slab, body, 0, unroll=8)


def _wait_slab_gather(src_hbm, dst, slab, sem):
    def body(r, carry):
        _slab_copy(src_hbm, 0, dst, r * slab, slab, sem).wait()
        return carry
    lax.fori_loop(0, dst.shape[0] // slab, body, 0, unroll=8)


def _expert_weight_copies(wgu_hbm, wdn_hbm, layer, expert, wgu_f32, wdn_f32, wslot, wsem):
    return (pltpu.make_async_copy(wgu_hbm.at[layer, expert], wgu_f32.at[wslot], wsem.at[wslot, 0]),
            pltpu.make_async_copy(wdn_hbm.at[layer, expert], wdn_f32.at[wslot], wsem.at[wslot, 1]))


def _experts_kernel(te_ref, nu_ref, rt_ref, first_ref, wslot_ref, next_ref, x_hbm, wgu_hbm, wdn_hbm, o_ref,
                    xbuf, sem, wgu_f32, wdn_f32, wsem, wgu_bf, wdn_bf, *, layer):
    t = pl.program_id(0)
    slab = wgu_bf.shape[0] // SLAB_WIDTH
    tm = xbuf.shape[1] // slab
    n_used = nu_ref[0]
    slot = lax.rem(t, 2)
    weights = functools.partial(_expert_weight_copies, wgu_hbm, wdn_hbm, layer,
                                wgu_f32=wgu_f32, wdn_f32=wdn_f32, wsem=wsem)

    @pl.when((t == 0) & (n_used > 0))
    def _():
        _start_slab_gather(x_hbm, rt_ref, 0, xbuf.at[0], slab, sem.at[0])
        for c in weights(te_ref[0], wslot=0):
            c.start()

    @pl.when(t + 1 < n_used)
    def _():
        _start_slab_gather(x_hbm, rt_ref, (t + 1) * tm, xbuf.at[1 - slot], slab, sem.at[1 - slot])

    @pl.when(t < n_used)
    def _():
        @pl.when(first_ref[t] == 1)
        def _():
            ws = wslot_ref[t]

            @pl.when(next_ref[t] >= 0)
            def _():
                for c in weights(next_ref[t], wslot=1 - ws):
                    c.start()

            for c in weights(te_ref[t], wslot=ws):
                c.wait()
            wgu_bf[...] = wgu_f32[ws].astype(BF16)
            wdn_bf[...] = wdn_f32[ws].astype(BF16)

        _wait_slab_gather(x_hbm, xbuf.at[slot], slab, sem.at[slot])
        x = _from_slabs(xbuf.at[slot], tm).astype(BF16)
        hg = jnp.dot(x, wgu_bf[...], preferred_element_type=F32)
        f = hg.shape[1] // 2
        a = hg[:, :f]
        hact = (a * _sigmoid(a)) * hg[:, f:]
        _to_slabs(o_ref, jnp.dot(hact.astype(BF16), wdn_bf[...], preferred_element_type=F32))

    @pl.when(t >= n_used)
    def _():
        o_ref[...] = jnp.zeros_like(o_ref)


def _experts(xn_slabs, plan, w_gu, w_dn, *, layer, tm):
    p = plan[2].shape[0]
    d, f2 = w_gu.shape[-2:]
    f = w_dn.shape[-2]
    slab = d // SLAB_WIDTH
    any_space = pl.BlockSpec(memory_space=pl.ANY)
    grid_spec = pltpu.PrefetchScalarGridSpec(
        num_scalar_prefetch=len(plan),
        grid=(p // tm,),
        in_specs=[any_space, any_space, any_space],
        out_specs=pl.BlockSpec((tm * slab, 128), lambda t, *_: (t, 0)),
        scratch_shapes=[pltpu.VMEM((2, tm * slab, 128), SLAB_DTYPE), pltpu.SemaphoreType.DMA((2,)),
                        pltpu.VMEM((2, d, f2), F32), pltpu.VMEM((2, f, d), F32), pltpu.SemaphoreType.DMA((2, 2)),
                        pltpu.VMEM((d, f2), BF16), pltpu.VMEM((f, d), BF16)],
    )
    return pl.pallas_call(
        functools.partial(_experts_kernel, layer=layer),
        out_shape=jax.ShapeDtypeStruct((p * slab, 128), SLAB_DTYPE),
        grid_spec=grid_spec,
        compiler_params=_params(("arbitrary",), 56),
        name="moe_experts",
    )(*plan, xn_slabs, w_gu, w_dn)


def _route(info, *, tm, slab):
    n = info.shape[0]
    e = info[:, :2].astype(jnp.int32).reshape(-1)
    onehot = (e[:, None] == jnp.arange(N_EXPERTS, dtype=jnp.int32)[None, :]).astype(jnp.int32)
    before = jnp.cumsum(onehot, axis=0) - onehot
    counts = jnp.sum(onehot, axis=0)
    padded = ((counts + tm - 1) // tm) * tm
    ends = jnp.cumsum(padded)
    starts = ends - padded
    pos = jnp.sum(onehot * (starts[None, :] + before), axis=1)
    p = ((2 * n + N_EXPERTS * (tm - 1)) // tm + 1) * tm
    row_start = jnp.zeros((p,), jnp.int32).at[pos].set((jnp.arange(2 * n, dtype=jnp.int32) // 2) * slab,
                                                       unique_indices=True, indices_are_sorted=False)
    tile_start = jnp.arange(p // tm, dtype=jnp.int32) * tm
    n_used = (ends[-1] // tm).astype(jnp.int32)
    used = tile_start < ends[-1]
    te = jnp.sum((tile_start[:, None] >= ends[None, :]).astype(jnp.int32), axis=1)
    last = jnp.sum((jnp.maximum(ends[-1] - 1, 0) >= ends).astype(jnp.int32))
    te = jnp.where(used, te, last).astype(jnp.int32)
    first = used & (te != jnp.concatenate([jnp.full((1,), -1, jnp.int32), te[:-1]]))
    wslot = lax.rem(jnp.cumsum(first.astype(jnp.int32)) - 1, 2)
    later = used[None, :] & (te[None, :] > te[:, None])
    nxt = jnp.min(jnp.where(later, te[None, :], N_EXPERTS), axis=1)
    nxt = jnp.where(nxt < N_EXPERTS, nxt, -1)
    plan = (te, n_used.reshape(1), row_start, first.astype(jnp.int32), wslot.astype(jnp.int32),
            nxt.astype(jnp.int32))
    pos2 = (pos * slab).reshape(n, 2)
    return plan, pos2[:, 0], pos2[:, 1]


def _ple_kernel(pos0_ref, pos1_ref, h_ref, info_ref, y_hbm, p_ref, gp_ref, wg_ref, wp_ref, *rest, final):
    if final:
        gf_ref, o_ref, ybuf, sem = rest
    else:
        o_ref, ybuf, sem = rest
    i = pl.program_id(0)
    tm, d = h_ref.shape
    slab = d // SLAB_WIDTH
    slot = lax.rem(i, 2)

    def start(tile, s):
        _start_slab_gather(y_hbm, pos0_ref, tile * tm, ybuf.at[s, 0], slab, sem.at[s])
        _start_slab_gather(y_hbm, pos1_ref, tile * tm, ybuf.at[s, 1], slab, sem.at[s])

    @pl.when(i == 0)
    def _():
        start(0, 0)

    @pl.when(i + 1 < pl.num_programs(0))
    def _():
        start(i + 1, 1 - slot)

    _wait_slab_gather(y_hbm, ybuf.at[slot, 0], slab, sem.at[slot])
    _wait_slab_gather(y_hbm, ybuf.at[slot, 1], slab, sem.at[slot])
    info = info_ref[...]
    y0 = _from_slabs(ybuf.at[slot, 0], tm)
    y1 = _from_slabs(ybuf.at[slot, 1], tm)
    h = h_ref[...] + (info[:, 2:3] * y0 + info[:, 3:4] * y1)
    hn = _rms(h, gp_ref[...]).astype(BF16)
    gate = _sigmoid(jnp.dot(hn, wg_ref[...], preferred_element_type=F32))
    proj = jnp.dot(p_ref[...].astype(BF16), wp_ref[...], preferred_element_type=F32)
    out = h + gate * proj
    if final:
        out = _rms(out, gf_ref[...])
    o_ref[...] = out


def _ple(h, info, y_sorted, pos0, pos1, *, p_emb, g_ple, w_gate, w_proj, g_final, tm, row0=0, n_rows=None):
    d = h.shape[1]
    n = h.shape[0] if n_rows is None else n_rows
    pd = p_emb.shape[1]
    r0 = row0 // tm
    row = lambda i, p0, p1: (r0 + i, 0)
    fix = lambda i, p0, p1: (0, 0)
    in_specs = [
        pl.BlockSpec((tm, d), row), pl.BlockSpec((tm, info.shape[1]), row), pl.BlockSpec(memory_space=pl.ANY),
        pl.BlockSpec((tm, pd), row), pl.BlockSpec((1, d), fix),
        pl.BlockSpec((d, d), fix), pl.BlockSpec((pd, d), fix),
    ]
    args = [h, info, y_sorted, p_emb, g_ple, w_gate, w_proj]
    if g_final is not None:
        in_specs.append(pl.BlockSpec((1, d), fix))
        args.append(g_final)
    grid_spec = pltpu.PrefetchScalarGridSpec(
        num_scalar_prefetch=2,
        grid=(n // tm,),
        in_specs=in_specs,
        out_specs=pl.BlockSpec((tm, d), lambda i, p0, p1: (i, 0)),
        scratch_shapes=[pltpu.VMEM((2, 2, tm * (d // SLAB_WIDTH), 128), SLAB_DTYPE), pltpu.SemaphoreType.DMA((2,))],
    )
    return pl.pallas_call(
        functools.partial(_ple_kernel, final=g_final is not None),
        out_shape=jax.ShapeDtypeStruct((n, d), F32),
        grid_spec=grid_spec,
        compiler_params=_params(("arbitrary",), 52),
        name="ple",
    )(pos0, pos1, *args)


def _linear_res_kernel(ap_ref, as_ref, w_ref, h_ref, o_ref, *, n_first):
    a = _pick_group(ap_ref, as_ref, n_first).astype(BF16)
    o_ref[...] = h_ref[...] + jnp.dot(a, w_ref[...], preferred_element_type=F32)


def _linear_res(a_p, a_s, w, h, *, tm):
    n, d = h.shape
    k = a_p.shape[1]
    n_first = a_p.shape[0]
    return pl.pallas_call(
        functools.partial(_linear_res_kernel, n_first=n_first),
        out_shape=jax.ShapeDtypeStruct((n, d), F32),
        grid=(n // tm,),
        in_specs=[*_two_group_specs(tm, k, n_first), pl.BlockSpec((k, d), lambda i: (0, 0)),
                  pl.BlockSpec((tm, d), lambda i: (i, 0))],
        out_specs=pl.BlockSpec((tm, d), lambda i: (i, 0)),
        compiler_params=_params(("parallel",), 40),
        name="attn_out",
    )(a_p, a_s, w, h)


def _band_attn_kernel(q0_ref, q1_ref, q2_ref, k_ref, v_ref, o_ref, og_ref, lse_ref, *, seq):
    q_refs = (q0_ref, q1_ref, q2_ref)
    row = lax.broadcasted_iota(jnp.int32, (BAND, BAND), 0)
    col = lax.broadcasted_iota(jnp.int32, (BAND, BAND), 1)
    cur_ok = col <= row
    prev_ok = col >= row
    row2 = lax.broadcasted_iota(jnp.int32, (BAND, 2 * BAND), 0)
    col2 = lax.broadcasted_iota(jnp.int32, (BAND, 2 * BAND), 1)
    both_ok = ((col2 < BAND) & (col2 >= row2)) | ((col2 >= BAND) & (col2 - BAND <= row2))
    del prev_ok
    nt = (((1,), (1,)), ((), ()))

    def rows(start, dil):
        return pl.ds(start, BAND) if dil == 1 else pl.ds(start, BAND, stride=dil)

    for gi, (win, dil) in enumerate(DIL_GROUPS):
        assert win // dil == BAND
        sub_len = seq // dil
        for r in range(dil):
            kp = vp = None
            for n in range(sub_len // BAND):
                sl = rows(r + dil * BAND * n, dil)
                qb = q_refs[gi][sl, :].astype(BF16)
                kc = k_ref[sl, :].astype(BF16)
                vc = v_ref[sl, :].astype(BF16)
                if kp is None:
                    keys, vals, ok = kc, vc, cur_ok
                else:
                    keys = jnp.concatenate([kp, kc], axis=0)
                    vals = jnp.concatenate([vp, vc], axis=0)
                    ok = both_ok
                s = jnp.where(ok, lax.dot_general(qb, keys, nt, preferred_element_type=F32), NEG_BIG)
                m = jnp.max(s, axis=-1, keepdims=True)
                p = jnp.exp(s - m)
                l = jnp.sum(p, axis=-1, keepdims=True)
                acc = jnp.dot(p.astype(BF16), vals, preferred_element_type=F32)
                og_ref[gi, sl, :] = acc / l
                lse_ref[gi, sl, :] = jnp.broadcast_to(m + jnp.log(l), (BAND, HEAD_DIM))
                kp, vp = kc, vc

    l0, l1, l2 = lse_ref[0], lse_ref[1], lse_ref[2]
    mx = jnp.maximum(jnp.maximum(l0, l1), l2)
    w0, w1, w2 = jnp.exp(l0 - mx), jnp.exp(l1 - mx), jnp.exp(l2 - mx)
    o_ref[...] = (w0 * og_ref[0] + w1 * og_ref[1] + w2 * og_ref[2]) / (w0 + w1 + w2)


def _band_attention(q, k, v, *, batch, seq, n_heads):
    blk = (seq, HEAD_DIM)
    qspec = lambda gi: pl.BlockSpec(blk, lambda b, h: (b, gi * n_heads + h))
    kv = pl.BlockSpec(blk, lambda b, h: (b, h))
    return pl.pallas_call(
        functools.partial(_band_attn_kernel, seq=seq),
        out_shape=jax.ShapeDtypeStruct((batch * seq, n_heads * HEAD_DIM), F32),
        grid=(batch, n_heads),
        in_specs=[qspec(0), qspec(1), qspec(2), kv, kv],
        out_specs=kv,
        scratch_shapes=[pltpu.VMEM((3, seq, HEAD_DIM), F32), pltpu.VMEM((3, seq, HEAD_DIM), F32)],
        compiler_params=_params(("parallel", "parallel"), 40),
        name="band_attention",
    )(q, q, q, k, v)


def _decode_rows(w_buf, dec_seq):
    max_dil = max(d for _, d in DIL_GROUPS)
    dense_from = w_buf
    for win, dil in DIL_GROUPS:
        if dil < max_dil:
            dense_from = min(dense_from, w_buf - win)
    dense_from = max((dense_from // max_dil) * max_dil, 0)
    return max_dil, dense_from


def _decode_bias(w_buf, dec_seq, key_pos, n_pad):
    bias = np.full((len(DIL_GROUPS) * dec_seq, n_pad), NEG_BIG, np.float32)
    for gi, (win, dil) in enumerate(DIL_GROUPS):
        for j in range(dec_seq):
            dist = (w_buf + j) - key_pos
            ok = (dist >= 0) & (dist <= win) & (dist % dil == 0)
            bias[gi * dec_seq + j, :len(key_pos)][ok] = 0.0
    return bias


def _decode_attn_kernel(q_ref, kn_ref, vn_ref, ka_ref, kb_ref, va_ref, vb_ref, bias_ref, o_ref, *,
                        n_heads, dec_seq, n_groups, n_pad):
    bias = bias_ref[...]
    nt = (((1,), (1,)), ((), ()))
    na = ka_ref.shape[0] * (ka_ref.shape[1] // n_heads)
    nb = kb_ref.shape[0] * (kb_ref.shape[1] // n_heads)
    tail = n_pad - na - nb - dec_seq

    def head_rows(a_ref, b_ref, n_ref, h):
        ra = a_ref.shape[1] // n_heads
        rb = b_ref.shape[1] // n_heads
        xa = a_ref[:, pl.ds(h, ra, stride=n_heads), :].reshape(na, HEAD_DIM)
        xb = b_ref[:, pl.ds(h, rb, stride=n_heads), :].reshape(nb, HEAD_DIM)
        xn = n_ref[:, h * HEAD_DIM:(h + 1) * HEAD_DIM]
        parts = [xa, xb, xn]
        if tail:
            parts.append(jnp.zeros((tail, HEAD_DIM), F32))
        return jnp.concatenate(parts, axis=0).astype(BF16)

    outs = []
    for h in range(n_heads):
        kh = head_rows(ka_ref, kb_ref, kn_ref, h)
        vh = head_rows(va_ref, vb_ref, vn_ref, h)
        qh = jnp.concatenate(
            [q_ref[:, (gi * n_heads + h) * HEAD_DIM:(gi * n_heads + h + 1) * HEAD_DIM] for gi in range(n_groups)],
            axis=0).astype(BF16)
        s = lax.dot_general(qh, kh, nt, preferred_element_type=F32) + bias
        m = jnp.max(s, axis=-1, keepdims=True)
        p = jnp.exp(s - m)
        l = jnp.sum(p, axis=-1, keepdims=True)
        og = jnp.dot(p.astype(BF16), vh, preferred_element_type=F32) / l
        lse = m + jnp.log(l)
        ls = [lse[gi * dec_seq:(gi + 1) * dec_seq] for gi in range(n_groups)]
        mx = functools.reduce(jnp.maximum, ls)
        ws = [jnp.exp(x - mx) for x in ls]
        num = sum(w * og[gi * dec_seq:(gi + 1) * dec_seq] for gi, w in enumerate(ws))
        outs.append(num / sum(ws))
    o_ref[...] = jnp.concatenate(outs, axis=1)


def _decode_attention(q, k_new, v_new, cache_k, cache_v, *, q_row0, n_heads):
    batch, w_buf = cache_k.shape[:2]
    dec_seq = k_new.shape[0] // batch
    n_groups = len(DIL_GROUPS)
    comb, dense_from = _decode_rows(w_buf, dec_seq)
    assert dec_seq <= comb and comb % dec_seq == 0 and dec_seq % 8 == 0 and w_buf % comb == 0
    n_comb = dense_from // comb
    n_dense = (w_buf - dense_from) // comb
    key_pos = np.concatenate([
        (np.arange(n_comb)[:, None] * comb + np.arange(dec_seq)[None, :]).reshape(-1),
        dense_from + np.arange(w_buf - dense_from),
        w_buf + np.arange(dec_seq)])
    n_pad = -(-len(key_pos) // 128) * 128
    bias = jnp.asarray(_decode_bias(w_buf, dec_seq, key_pos, n_pad))
    ck = cache_k.reshape(batch, w_buf // comb, comb * n_heads, HEAD_DIM)
    cv = cache_v.reshape(batch, w_buf // comb, comb * n_heads, HEAD_DIM)
    tok = lambda width, row0=0: pl.BlockSpec((dec_seq, width), lambda b: (row0 // dec_seq + b, 0))
    comb_spec = pl.BlockSpec((None, n_comb, dec_seq * n_heads, HEAD_DIM), lambda b: (b, 0, 0, 0))
    dense_spec = pl.BlockSpec((None, n_dense, comb * n_heads, HEAD_DIM), lambda b: (b, n_comb // n_dense, 0, 0))
    assert n_comb % n_dense == 0
    return pl.pallas_call(
        functools.partial(_decode_attn_kernel, n_heads=n_heads, dec_seq=dec_seq, n_groups=n_groups, n_pad=n_pad),
        out_shape=jax.ShapeDtypeStruct((batch * dec_seq, n_heads * HEAD_DIM), F32),
        grid=(batch,),
        in_specs=[tok(q.shape[1], q_row0), tok(k_new.shape[1]), tok(v_new.shape[1]),
                  comb_spec, dense_spec, comb_spec, dense_spec,
                  pl.BlockSpec(bias.shape, lambda b: (0, 0))],
        out_specs=pl.BlockSpec((dec_seq, n_heads * HEAD_DIM), lambda b: (b, 0)),
        compiler_params=_params(("parallel",), 48),
        name="decode_attention",
    )(q, k_new, v_new, ck, ck, cv, cv, bias)


def _moe_and_ple(h, p_emb, layer, w, g_final, *, tm_moe, split=None):
    n, d = h.shape
    lanes = 128
    w_router = jnp.zeros((d, lanes), F32)
    w_router = w_router.at[:, :N_EXPERT_GROUPS].set(w["moe_w_rg"][layer])
    w_router = w_router.at[:, N_EXPERT_GROUPS:N_EXPERT_GROUPS + N_EXPERTS].set(w["moe_w_re"][layer])
    xn, info = _router(h, w["norm_ffn"][layer][None, :], w_router, tm=512)
    plan, pos0, pos1 = _route(info, tm=tm_moe, slab=d // SLAB_WIDTH)
    y_sorted = _experts(xn, plan, w["moe_w_gu"], w["moe_w_dn"], layer=layer, tm=tm_moe)
    ple = functools.partial(_ple, h, info, y_sorted, p_emb=p_emb, g_ple=w["norm_ple"][layer][None, :],
                            w_gate=w["ple_w_gate"][layer].astype(BF16), w_proj=w["ple_w_proj"][layer].astype(BF16),
                            g_final=g_final, tm=256)
    if split is None:
        return ple(pos0, pos1)
    return (ple(pos0[:split], pos1[:split], row0=0, n_rows=split),
            ple(pos0[split:], pos1[split:], row0=split, n_rows=n - split))


def kernel(x_prompt, x_sample, p_prompt, p_sample, state_gla, cache_k, cache_v, norm_mix, norm_ffn, norm_ple,
           norm_kv, norm_final, gla_w_in, gla_w_a2, gla_b_a, gla_g_out, gla_w_out, w_kv, dil_w_q, dil_w_out,
           moe_w_rg, moe_w_re, moe_w_gu, moe_w_dn, ple_w_gate, ple_w_proj):
    w = dict(norm_ffn=norm_ffn, norm_ple=norm_ple, moe_w_rg=moe_w_rg, moe_w_re=moe_w_re, moe_w_gu=moe_w_gu,
             moe_w_dn=moe_w_dn, ple_w_gate=ple_w_gate, ple_w_proj=ple_w_proj)
    bp, tp, d = x_prompt.shape
    bs, ts, _ = x_sample.shape
    n_p, n_s = bp * tp, bs * ts
    depth = p_prompt.shape[0]
    assert depth == 2 and state_gla.shape[0] == 1
    heads, dk, dv = state_gla.shape[2:]
    n_kv = cache_k.shape[2]
    past_len = cache_k.shape[1]
    qk_w, v_w = heads * dk, heads * dv
    lowrank = gla_w_a2.shape[1]

    x = jnp.concatenate([x_prompt.reshape(n_p, d), x_sample.reshape(n_s, d)], axis=0)
    p_emb = jnp.concatenate([p_prompt.reshape(depth, n_p, -1), p_sample.reshape(depth, n_s, -1)], axis=1)

    half = HEAD_DIM // 2
    inv = ROPE_THETA ** (-jnp.arange(half, dtype=F32) / half)
    pos = jnp.concatenate([jnp.tile(jnp.arange(tp), bp), jnp.tile(past_len + jnp.arange(ts), bs)]).astype(F32)
    ang = pos[:, None] * inv[None, :]
    rope = (jnp.concatenate([jnp.cos(ang), jnp.cos(ang)], axis=1),
            jnp.concatenate([-jnp.sin(ang), jnp.sin(ang)], axis=1))

    g_mix0 = norm_mix[0][None, :]
    w_in = gla_w_in[0].astype(BF16)
    main_w = 2 * qk_w + 2 * v_w
    tm_wide = 1024 if (n_p + n_s) % 1024 == 0 else 512
    z = _norm_linear(x, g_mix0, w_in, col_start=0, n_cols=main_w, tn=1024, tm=tm_wide)
    w_a1 = jnp.zeros((d, 128), BF16).at[:, :lowrank].set(w_in[:, main_w:])
    w_a2 = jnp.zeros((128, qk_w), F32).at[:lowrank].set(gla_w_a2[0])
    g = _gla_gate(x, g_mix0, w_a1, w_a2, gla_b_a[0][None, :], tm=512)
    c_p = int(np.gcd(tp, GLA_CHUNK))
    c_s = int(np.gcd(ts, GLA_CHUNK))
    gla = functools.partial(_gla_scan, z, g, heads=heads, dk=dk, dv=dv)
    o_p, sg_p = gla(None, row0=0, batch=bp, seq=tp, c=c_p, lanes=2 if bp % 2 == 0 else 1)
    o_s, sg_s = gla(state_gla[0], row0=n_p, batch=bs, seq=ts, c=c_s, lanes=2 if bs % 2 == 0 else 1)
    h = _gla_out(o_p, o_s, z, x, gla_g_out[0][None, :], gla_w_out[0].astype(BF16), heads=heads, tm=256)
    h = _moe_and_ple(h, p_emb[0], 0, w, None, tm_moe=512)

    w_kv_b = w_kv.astype(BF16)
    kv_w = n_kv * HEAD_DIM
    g_kv = norm_kv[None, :]
    kv_lin = functools.partial(_norm_linear, h, g_kv, w_kv_b, n_cols=kv_w, tn=kv_w, tm=512)
    k_p = kv_lin(col_start=0, rope=rope, row0=0, n_rows=n_p)
    k_s = kv_lin(col_start=0, rope=rope, row0=n_p, n_rows=n_s)
    v_p = kv_lin(col_start=kv_w, row0=0, n_rows=n_p)
    v_s = kv_lin(col_start=kv_w, row0=n_p, n_rows=n_s)
    q_all = _norm_linear(h, norm_mix[1][None, :], dil_w_q[0].astype(BF16), col_start=0,
                         n_cols=dil_w_q.shape[2], tn=kv_w, tm=tm_wide, rope=rope, scale=HEAD_DIM ** -0.5)

    a_p = _band_attention(q_all, k_p, v_p, batch=bp, seq=tp, n_heads=n_kv)
    a_s = _decode_attention(q_all, k_s, v_s, cache_k, cache_v, q_row0=n_p, n_heads=n_kv)
    h = _linear_res(a_p, a_s, dil_w_out[0].astype(BF16), h, tm=512)
    y_p, y_s = _moe_and_ple(h, p_emb[1], 1, w, norm_final[None, :], tm_moe=256, split=n_p)

    rows = min(past_len, tp)
    as_heads = lambda a, b, t: a.reshape(b, t, n_kv, HEAD_DIM)
    return (y_p.reshape(bp, tp, d), y_s.reshape(bs, ts, d), sg_p[None], sg_s[None],
            as_heads(k_p, bp, tp)[:, tp - rows:], as_heads(v_p, bp, tp)[:, tp - rows:],
            as_heads(k_s, bs, ts), as_heads(v_s, bs, ts))
```

```python
import functools

import numpy as np
import jax
import jax.numpy as jnp
from jax import lax
from jax.experimental import pallas as pl
from jax.experimental.pallas import tpu as pltpu

F32 = jnp.float32
BF16 = jnp.bfloat16
HIGHEST = lax.Precision.HIGHEST

NORM_EPS = 1e-6
GLA_TAU = 16.0
GLA_CHUNK = 64
GLA_PAD = 128
HEAD_DIM = 128
BAND = 128
DIL_GROUPS = ((128, 1), (512, 4), (2048, 16))
ROPE_THETA = 10000.0
N_EXPERT_GROUPS = 4
EXPERTS_PER_GROUP = 8
N_EXPERTS = N_EXPERT_GROUPS * EXPERTS_PER_GROUP
NEG_BIG = -1e30
MIB = 1024 * 1024


def _params(semantics, vmem_mib):
    return pltpu.CompilerParams(dimension_semantics=semantics, vmem_limit_bytes=vmem_mib * MIB)


def _rms(x, gain):
    var = jnp.mean(x * x, axis=-1, keepdims=True)
    return x * lax.rsqrt(var + NORM_EPS) * gain


def _sigmoid(x):
    return 1.0 / (1.0 + jnp.exp(-x))


def _norm_linear_kernel(x_ref, g_ref, w_ref, *rest, rope, scale):
    if rope:
        cos_ref, sin_ref, o_ref, xn_ref = rest
    else:
        o_ref, xn_ref = rest

    @pl.when(pl.program_id(1) == 0)
    def _():
        xn_ref[...] = _rms(x_ref[...], g_ref[...]).astype(BF16)

    acc = jnp.dot(xn_ref[...], w_ref[...], preferred_element_type=F32)
    if rope:
        cos = cos_ref[...]
        sin = sin_ref[...]
        parts = []
        for c in range(acc.shape[1] // HEAD_DIM):
            y = acc[:, c * HEAD_DIM:(c + 1) * HEAD_DIM]
            parts.append(y * cos + pltpu.roll(y, HEAD_DIM // 2, 1) * sin)
        acc = parts[0] if len(parts) == 1 else jnp.concatenate(parts, axis=1)
    if scale != 1.0:
        acc = acc * scale
    o_ref[...] = acc


def _norm_linear(x, gain, w, *, col_start, n_cols, tn, tm, rope=None, scale=1.0, row0=0, n_rows=None):
    k = x.shape[1]
    n = x.shape[0] if n_rows is None else n_rows
    grid = (n // tm, n_cols // tn)
    c0 = col_start // tn
    r0 = row0 // tm
    in_specs = [
        pl.BlockSpec((tm, k), lambda i, j: (r0 + i, 0)),
        pl.BlockSpec((1, k), lambda i, j: (0, 0)),
        pl.BlockSpec((k, tn), lambda i, j: (0, c0 + j)),
    ]
    args = [x, gain, w]
    if rope is not None:
        in_specs += [pl.BlockSpec((tm, HEAD_DIM), lambda i, j: (r0 + i, 0))] * 2
        args += list(rope)
    return pl.pallas_call(
        functools.partial(_norm_linear_kernel, rope=rope is not None, scale=scale),
        out_shape=jax.ShapeDtypeStruct((n, n_cols), F32),
        grid=grid,
        in_specs=in_specs,
        out_specs=pl.BlockSpec((tm, tn), lambda i, j: (i, j)),
        scratch_shapes=[pltpu.VMEM((tm, k), BF16)],
        compiler_params=_params(("parallel", "arbitrary"), 48),
        name="norm_linear",
    )(*args)


def _gla_gate_kernel(x_ref, gn_ref, w1_ref, w2_ref, b_ref, o_ref):
    xn = _rms(x_ref[...], gn_ref[...]).astype(BF16)
    a1 = jnp.dot(xn, w1_ref[...], preferred_element_type=F32)
    pre = jnp.dot(a1, w2_ref[...], precision=HIGHEST, preferred_element_type=F32) + b_ref[...]
    o_ref[...] = (jnp.minimum(pre, 0.0) - jnp.log1p(jnp.exp(-jnp.abs(pre)))) * (1.0 / GLA_TAU)


def _gla_gate(x, gain, w1, w2, b_a, *, tm):
    n, k = x.shape
    r, qk = w2.shape
    return pl.pallas_call(
        _gla_gate_kernel,
        out_shape=jax.ShapeDtypeStruct((n, qk), F32),
        grid=(n // tm,),
        in_specs=[
            pl.BlockSpec((tm, k), lambda i: (i, 0)),
            pl.BlockSpec((1, k), lambda i: (0, 0)),
            pl.BlockSpec((k, r), lambda i: (0, 0)),
            pl.BlockSpec((r, qk), lambda i: (0, 0)),
            pl.BlockSpec((1, qk), lambda i: (0, 0)),
        ],
        out_specs=pl.BlockSpec((tm, qk), lambda i: (i, 0)),
        compiler_params=_params(("parallel",), 32),
        name="gla_gate",
    )(x, gain, w1, w2, b_a)


def _gla_kernel(*refs, c, heads, lanes, has_s0, q_scale):
    lane_refs = [refs[4 * i:4 * i + 4] for i in range(lanes)]
    s0_ref = refs[4 * lanes] if has_s0 else None
    o_ref, so_ref, s_scr = refs[-3:]
    n = pl.program_id(1)
    dk = lane_refs[0][0].shape[1] // heads
    dv = lane_refs[0][2].shape[1] // heads

    @pl.when(n == 0)
    def _():
        if has_s0:
            s_scr[...] = s0_ref[...]
        else:
            s_scr[...] = jnp.zeros_like(s_scr)

    def pad(a):
        if c == GLA_PAD:
            return a
        return jnp.concatenate([a, jnp.zeros((GLA_PAD - c, a.shape[1]), a.dtype)], axis=0)

    row = lax.broadcasted_iota(jnp.int32, (GLA_PAD, GLA_PAD), 0)
    col = lax.broadcasted_iota(jnp.int32, (GLA_PAD, GLA_PAD), 1)
    tri = jnp.where(row >= col, 1.0, 0.0).astype(F32)
    rc = lax.broadcasted_iota(jnp.int32, (c, c), 0)
    cc = lax.broadcasted_iota(jnp.int32, (c, c), 1)
    for lane, (q_ref, k_ref, v_ref, g_ref) in enumerate(lane_refs):
        bp = jnp.dot(tri, pad(g_ref[...]), precision=HIGHEST, preferred_element_type=F32)
        b = bp[:c]
        k = k_ref[...]
        v = v_ref[...]
        qe = (q_ref[...] * q_scale * jnp.exp(b)).astype(BF16)
        ke = (k * jnp.exp(-b)).astype(BF16)
        vb = v.astype(BF16)
        k_t = pad(k).T
        b_t = bp.T
        b_last = b_t[:, c - 1:c]
        kd_t = (k_t * jnp.exp(b_last - b_t)).astype(BF16)
        decay = jnp.exp(b_last)
        vp = pad(v).astype(BF16)
        for h in range(heads):
            ks = slice(h * dk, (h + 1) * dk)
            vs = slice(h * dv, (h + 1) * dv)
            s_old = s_scr[lane, h]
            o = jnp.dot(qe[:, ks], s_old.astype(BF16), preferred_element_type=F32)
            a = lax.dot_general(qe[:, ks], ke[:, ks], (((1,), (1,)), ((), ())), preferred_element_type=F32)
            a = jnp.where(rc >= cc, a, 0.0)
            o_ref[lane, :, vs] = o + jnp.dot(a.astype(BF16), vb[:, vs], preferred_element_type=F32)
            s_scr[lane, h] = decay[ks] * s_old + jnp.dot(kd_t[ks], vp[:, vs], preferred_element_type=F32)

    @pl.when(n == pl.num_programs(1) - 1)
    def _():
        so_ref[...] = s_scr[...]


def _gla_scan(z, g, s0, *, row0, batch, seq, heads, dk, dv, c, lanes):
    nchunk = seq // c
    rb0 = row0 // c
    qk_w, v_w = heads * dk, heads * dv
    in_specs, args = [], []
    for lane in range(lanes):
        rows = lambda b, n, lane=lane: rb0 + (b * lanes + lane) * nchunk + n
        in_specs += [
            pl.BlockSpec((c, qk_w), lambda b, n, rows=rows: (rows(b, n), 0)),
            pl.BlockSpec((c, qk_w), lambda b, n, rows=rows: (rows(b, n), 1)),
            pl.BlockSpec((c, v_w), lambda b, n, rows=rows: (rows(b, n), 2 * qk_w // v_w)),
            pl.BlockSpec((c, qk_w), lambda b, n, rows=rows: (rows(b, n), 0)),
        ]
        args += [z, z, z, g]
    state_spec = pl.BlockSpec((lanes, heads, dk, dv), lambda b, n: (b, 0, 0, 0))
    if s0 is not None:
        in_specs.append(state_spec)
        args.append(s0)
    o, s_out = pl.pallas_call(
        functools.partial(_gla_kernel, c=c, heads=heads, lanes=lanes, has_s0=s0 is not None,
                          q_scale=float(dk) ** -0.5),
        out_shape=(jax.ShapeDtypeStruct((batch, seq, v_w), F32),
                   jax.ShapeDtypeStruct((batch, heads, dk, dv), F32)),
        grid=(batch // lanes, nchunk),
        in_specs=in_specs,
        out_specs=(pl.BlockSpec((lanes, c, v_w), lambda b, n: (b, n, 0)), state_spec),
        scratch_shapes=[pltpu.VMEM((lanes, heads, dk, dv), F32)],
        compiler_params=_params(("parallel", "arbitrary"), 48),
        name="gla_scan",
    )(*args)
    return o.reshape(batch * seq, v_w), s_out


def _two_group_specs(tm, width, n_first):
    t_first = n_first // tm
    return (pl.BlockSpec((tm, width), lambda i: (jnp.minimum(i, t_first - 1), 0)),
            pl.BlockSpec((tm, width), lambda i: (jnp.maximum(i - t_first, 0), 0)))


def _pick_group(first_ref, second_ref, n_first):
    in_first = pl.program_id(0) < n_first // first_ref.shape[0]
    return jnp.where(in_first, first_ref[...], second_ref[...])


def _gla_out_kernel(op_ref, os_ref, r_ref, x_ref, go_ref, w_ref, h_ref, *, heads, n_first):
    o = _pick_group(op_ref, os_ref, n_first)
    dv = o.shape[1] // heads
    go = go_ref[...]
    parts = [_rms(o[:, h * dv:(h + 1) * dv], go) for h in range(heads)]
    on = jnp.concatenate(parts, axis=1)
    r = r_ref[...]
    y = (on * (r * _sigmoid(r))).astype(BF16)
    h_ref[...] = x_ref[...] + jnp.dot(y, w_ref[...], preferred_element_type=F32)


def _gla_out(o_p, o_s, z, x, g_out, w_out, *, heads, tm):
    n, d = x.shape
    vw = o_p.shape[1]
    n_first = o_p.shape[0]
    rblk = (z.shape[1] - vw) // vw
    return pl.pallas_call(
        functools.partial(_gla_out_kernel, heads=heads, n_first=n_first),
        out_shape=jax.ShapeDtypeStruct((n, d), F32),
        grid=(n // tm,),
        in_specs=[
            *_two_group_specs(tm, vw, n_first),
            pl.BlockSpec((tm, vw), lambda i: (i, rblk)),
            pl.BlockSpec((tm, d), lambda i: (i, 0)),
            pl.BlockSpec((1, vw // heads), lambda i: (0, 0)),
            pl.BlockSpec((vw, d), lambda i: (0, 0)),
        ],
        out_specs=pl.BlockSpec((tm, d), lambda i: (i, 0)),
        compiler_params=_params(("parallel",), 44),
        name="gla_out",
    )(o_p, o_s, z, x, g_out, w_out)


SLAB_WIDTH = 128
SLAB_DTYPE = F32


def _from_slabs(ref, rows):
    slab = ref.shape[0] // rows
    return jnp.concatenate([ref[pl.ds(s, rows, stride=slab), :] for s in range(slab)], axis=1)


def _to_slabs(ref, val):
    rows = val.shape[0]
    slab = ref.shape[0] // rows
    for s in range(slab):
        ref[pl.ds(s, rows, stride=slab), :] = val[:, s * 128:(s + 1) * 128]


def _router_kernel(x_ref, g_ref, w_ref, xn_ref, info_ref):
    xn = _rms(x_ref[...], g_ref[...])
    _to_slabs(xn_ref, xn)
    lg = jnp.dot(xn, w_ref[...], precision=HIGHEST, preferred_element_type=F32)
    lane = lax.broadcasted_iota(jnp.int32, lg.shape, 1).astype(F32)
    far = float(lg.shape[1])
    is_grp = lane < N_EXPERT_GROUPS
    lgm = jnp.where(is_grp, lg, NEG_BIG)
    gmax = jnp.max(lgm, axis=-1, keepdims=True)
    gsum = jnp.sum(jnp.where(is_grp, jnp.exp(lg - gmax), 0.0), axis=-1, keepdims=True)
    p_top = 1.0 / gsum
    g_top = jnp.min(jnp.where(lgm == gmax, lane, far), axis=-1, keepdims=True)
    lo = N_EXPERT_GROUPS + g_top * EXPERTS_PER_GROUP
    in_grp = (lane >= lo) & (lane < lo + EXPERTS_PER_GROUP)
    le = jnp.where(in_grp, lg, NEG_BIG)
    m0 = jnp.max(le, axis=-1, keepdims=True)
    i0 = jnp.min(jnp.where(le == m0, lane, far), axis=-1, keepdims=True)
    le1 = jnp.where(lane == i0, NEG_BIG, le)
    m1 = jnp.max(le1, axis=-1, keepdims=True)
    i1 = jnp.min(jnp.where(le1 == m1, lane, far), axis=-1, keepdims=True)
    t = jnp.exp(m1 - m0)
    w0 = p_top / (1.0 + t)
    w1 = p_top * t / (1.0 + t)
    info = jnp.where(lane == 0.0, i0 - N_EXPERT_GROUPS,
                     jnp.where(lane == 1.0, i1 - N_EXPERT_GROUPS,
                               jnp.where(lane == 2.0, w0, jnp.where(lane == 3.0, w1, 0.0))))
    info_ref[...] = info


def _router(x, gain, w_router, *, tm):
    n, k = x.shape
    lanes = w_router.shape[1]
    slab = k // SLAB_WIDTH
    return pl.pallas_call(
        _router_kernel,
        out_shape=(jax.ShapeDtypeStruct((n * slab, 128), SLAB_DTYPE), jax.ShapeDtypeStruct((n, lanes), F32)),
        grid=(n // tm,),
        in_specs=[
            pl.BlockSpec((tm, k), lambda i: (i, 0)),
            pl.BlockSpec((1, k), lambda i: (0, 0)),
            pl.BlockSpec((k, lanes), lambda i: (0, 0)),
        ],
        out_specs=(pl.BlockSpec((tm * slab, 128), lambda i: (i, 0)), pl.BlockSpec((tm, lanes), lambda i: (i, 0))),
        compiler_params=_params(("parallel",), 32),
        name="moe_router",
    )(x, gain, w_router)


def _slab_copy(src_hbm, src_row, dst, dst_row, slab, sem):
    src_row = pl.multiple_of(src_row, slab)
    dst_row = pl.multiple_of(dst_row, slab)
    return pltpu.make_async_copy(src_hbm.at[pl.ds(src_row, slab)], dst.at[pl.ds(dst_row, slab)], sem)


def _start_slab_gather(src_hbm, idx_ref, base, dst, slab, sem):
    def body(r, carry):
        _slab_copy(src_hbm, idx_ref[base + r], dst, r * slab, slab, sem).start()
        return carry
    lax.fori_loop(0, dst.shape[0] // slab, body, 0, unroll=8)


def _wait_slab_gather(src_hbm, dst, slab, sem):
    def body(r, carry):
        _slab_copy(src_hbm, 0, dst, r * slab, slab, sem).wait()
        return carry
    lax.fori_loop(0, dst.shape[0] // slab, body, 0, unroll=8)


def _expert_weight_copies(wgu_hbm, wdn_hbm, layer, expert, wgu_f32, wdn_f32, wslot, wsem):
    return (pltpu.make_async_copy(wgu_hbm.at[layer, expert], wgu_f32.at[wslot], wsem.at[wslot, 0]),
            pltpu.make_async_copy(wdn_hbm.at[layer, expert], wdn_f32.at[wslot], wsem.at[wslot, 1]))


def _experts_kernel(te_ref, nu_ref, rt_ref, first_ref, wslot_ref, next_ref, x_hbm, wgu_hbm, wdn_hbm, o_ref,
                    xbuf, sem, wgu_f32, wdn_f32, wsem, wgu_bf, wdn_bf, *, layer):
    t = pl.program_id(0)
    slab = wgu_bf.shape[0] // SLAB_WIDTH
    tm = xbuf.shape[1] // slab
    n_used = nu_ref[0]
    slot = lax.rem(t, 2)
    weights = functools.partial(_expert_weight_copies, wgu_hbm, wdn_hbm, layer,
                                wgu_f32=wgu_f32, wdn_f32=wdn_f32, wsem=wsem)

    @pl.when((t == 0) & (n_used > 0))
    def _():
        _start_slab_gather(x_hbm, rt_ref, 0, xbuf.at[0], slab, sem.at[0])
        for c in weights(te_ref[0], wslot=0):
            c.start()

    @pl.when(t + 1 < n_used)
    def _():
        _start_slab_gather(x_hbm, rt_ref, (t + 1) * tm, xbuf.at[1 - slot], slab, sem.at[1 - slot])

    @pl.when(t < n_used)
    def _():
        @pl.when(first_ref[t] == 1)
        def _():
            ws = wslot_ref[t]

            @pl.when(next_ref[t] >= 0)
            def _():
                for c in weights(next_ref[t], wslot=1 - ws):
                    c.start()

            for c in weights(te_ref[t], wslot=ws):
                c.wait()
            wgu_bf[...] = wgu_f32[ws].astype(BF16)
            wdn_bf[...] = wdn_f32[ws].astype(BF16)

        _wait_slab_gather(x_hbm, xbuf.at[slot], slab, sem.at[slot])
        x = _from_slabs(xbuf.at[slot], tm).astype(BF16)
        hg = jnp.dot(x, wgu_bf[...], preferred_element_type=F32)
        f = hg.shape[1] // 2
        a = hg[:, :f]
        hact = (a * _sigmoid(a)) * hg[:, f:]
        _to_slabs(o_ref, jnp.dot(hact.astype(BF16), wdn_bf[...], preferred_element_type=F32))

    @pl.when(t >= n_used)
    def _():
        o_ref[...] = jnp.zeros_like(o_ref)


def _experts(xn_slabs, plan, w_gu, w_dn, *, layer, tm):
    p = plan[2].shape[0]
    d, f2 = w_gu.shape[-2:]
    f = w_dn.shape[-2]
    slab = d // SLAB_WIDTH
    any_space = pl.BlockSpec(memory_space=pl.ANY)
    grid_spec = pltpu.PrefetchScalarGridSpec(
        num_scalar_prefetch=len(plan),
        grid=(p // tm,),
        in_specs=[any_space, any_space, any_space],
        out_specs=pl.BlockSpec((tm * slab, 128), lambda t, *_: (t, 0)),
        scratch_shapes=[pltpu.VMEM((2, tm * slab, 128), SLAB_DTYPE), pltpu.SemaphoreType.DMA((2,)),
                        pltpu.VMEM((2, d, f2), F32), pltpu.VMEM((2, f, d), F32), pltpu.SemaphoreType.DMA((2, 2)),
                        pltpu.VMEM((d, f2), BF16), pltpu.VMEM((f, d), BF16)],
    )
    return pl.pallas_call(
        functools.partial(_experts_kernel, layer=layer),
        out_shape=jax.ShapeDtypeStruct((p * slab, 128), SLAB_DTYPE),
        grid_spec=grid_spec,
        compiler_params=_params(("arbitrary",), 56),
        name="moe_experts",
    )(*plan, xn_slabs, w_gu, w_dn)


def _route(info, *, tm, slab):
    n = info.shape[0]
    e = info[:, :2].astype(jnp.int32).reshape(-1)
    onehot = (e[:, None] == jnp.arange(N_EXPERTS, dtype=jnp.int32)[None, :]).astype(jnp.int32)
    before = jnp.cumsum(onehot, axis=0) - onehot
    counts = jnp.sum(onehot, axis=0)
    padded = ((counts + tm - 1) // tm) * tm
    ends = jnp.cumsum(padded)
    starts = ends - padded
    pos = jnp.sum(onehot * (starts[None, :] + before), axis=1)
    p = ((2 * n + N_EXPERTS * (tm - 1)) // tm + 1) * tm
    row_start = jnp.zeros((p,), jnp.int32).at[pos].set((jnp.arange(2 * n, dtype=jnp.int32) // 2) * slab,
                                                       unique_indices=True, indices_are_sorted=False)
    tile_start = jnp.arange(p // tm, dtype=jnp.int32) * tm
    n_used = (ends[-1] // tm).astype(jnp.int32)
    used = tile_start < ends[-1]
    te = jnp.sum((tile_start[:, None] >= ends[None, :]).astype(jnp.int32), axis=1)
    last = jnp.sum((jnp.maximum(ends[-1] - 1, 0) >= ends).astype(jnp.int32))
    te = jnp.where(used, te, last).astype(jnp.int32)
    first = used & (te != jnp.concatenate([jnp.full((1,), -1, jnp.int32), te[:-1]]))
    wslot = lax.rem(jnp.cumsum(first.astype(jnp.int32)) - 1, 2)
    later = used[None, :] & (te[None, :] > te[:, None])
    nxt = jnp.min(jnp.where(later, te[None, :], N_EXPERTS), axis=1)
    nxt = jnp.where(nxt < N_EXPERTS, nxt, -1)
    plan = (te, n_used.reshape(1), row_start, first.astype(jnp.int32), wslot.astype(jnp.int32),
            nxt.astype(jnp.int32))
    pos2 = (pos * slab).reshape(n, 2)
    return plan, pos2[:, 0], pos2[:, 1]


def _ple_kernel(pos0_ref, pos1_ref, h_ref, info_ref, y_hbm, p_ref, gp_ref, wg_ref, wp_ref, *rest, final):
    if final:
        gf_ref, o_ref, ybuf, sem = rest
    else:
        o_ref, ybuf, sem = rest
    i = pl.program_id(0)
    tm, d = h_ref.shape
    slab = d // SLAB_WIDTH
    slot = lax.rem(i, 2)

    def start(tile, s):
        _start_slab_gather(y_hbm, pos0_ref, tile * tm, ybuf.at[s, 0], slab, sem.at[s])
        _start_slab_gather(y_hbm, pos1_ref, tile * tm, ybuf.at[s, 1], slab, sem.at[s])

    @pl.when(i == 0)
    def _():
        start(0, 0)

    @pl.when(i + 1 < pl.num_programs(0))
    def _():
        start(i + 1, 1 - slot)

    _wait_slab_gather(y_hbm, ybuf.at[slot, 0], slab, sem.at[slot])
    _wait_slab_gather(y_hbm, ybuf.at[slot, 1], slab, sem.at[slot])
    info = info_ref[...]
    y0 = _from_slabs(ybuf.at[slot, 0], tm)
    y1 = _from_slabs(ybuf.at[slot, 1], tm)
    h = h_ref[...] + (info[:, 2:3] * y0 + info[:, 3:4] * y1)
    hn = _rms(h, gp_ref[...]).astype(BF16)
    gate = _sigmoid(jnp.dot(hn, wg_ref[...], preferred_element_type=F32))
    proj = jnp.dot(p_ref[...].astype(BF16), wp_ref[...], preferred_element_type=F32)
    out = h + gate * proj
    if final:
        out = _rms(out, gf_ref[...])
    o_ref[...] = out


def _ple(h, info, y_sorted, pos0, pos1, *, p_emb, g_ple, w_gate, w_proj, g_final, tm, row0=0, n_rows=None):
    d = h.shape[1]
    n = h.shape[0] if n_rows is None else n_rows
    pd = p_emb.shape[1]
    r0 = row0 // tm
    row = lambda i, p0, p1: (r0 + i, 0)
    fix = lambda i, p0, p1: (0, 0)
    in_specs = [
        pl.BlockSpec((tm, d), row), pl.BlockSpec((tm, info.shape[1]), row), pl.BlockSpec(memory_space=pl.ANY),
        pl.BlockSpec((tm, pd), row), pl.BlockSpec((1, d), fix),
        pl.BlockSpec((d, d), fix), pl.BlockSpec((pd, d), fix),
    ]
    args = [h, info, y_sorted, p_emb, g_ple, w_gate, w_proj]
    if g_final is not None:
        in_specs.append(pl.BlockSpec((1, d), fix))
        args.append(g_final)
    grid_spec = pltpu.PrefetchScalarGridSpec(
        num_scalar_prefetch=2,
        grid=(n // tm,),
        in_specs=in_specs,
        out_specs=pl.BlockSpec((tm, d), lambda i, p0, p1: (i, 0)),
        scratch_shapes=[pltpu.VMEM((2, 2, tm * (d // SLAB_WIDTH), 128), SLAB_DTYPE), pltpu.SemaphoreType.DMA((2,))],
    )
    return pl.pallas_call(
        functools.partial(_ple_kernel, final=g_final is not None),
        out_shape=jax.ShapeDtypeStruct((n, d), F32),
        grid_spec=grid_spec,
        compiler_params=_params(("arbitrary",), 52),
        name="ple",
    )(pos0, pos1, *args)


def _linear_res_kernel(ap_ref, as_ref, w_ref, h_ref, o_ref, *, n_first):
    a = _pick_group(ap_ref, as_ref, n_first).astype(BF16)
    o_ref[...] = h_ref[...] + jnp.dot(a, w_ref[...], preferred_element_type=F32)


def _linear_res(a_p, a_s, w, h, *, tm):
    n, d = h.shape
    k = a_p.shape[1]
    n_first = a_p.shape[0]
    return pl.pallas_call(
        functools.partial(_linear_res_kernel, n_first=n_first),
        out_shape=jax.ShapeDtypeStruct((n, d), F32),
        grid=(n // tm,),
        in_specs=[*_two_group_specs(tm, k, n_first), pl.BlockSpec((k, d), lambda i: (0, 0)),
                  pl.BlockSpec((tm, d), lambda i: (i, 0))],
        out_specs=pl.BlockSpec((tm, d), lambda i: (i, 0)),
        compiler_params=_params(("parallel",), 40),
        name="attn_out",
    )(a_p, a_s, w, h)


def _band_attn_kernel(q0_ref, q1_ref, q2_ref, k_ref, v_ref, o_ref, og_ref, lse_ref, *, seq):
    q_refs = (q0_ref, q1_ref, q2_ref)
    row = lax.broadcasted_iota(jnp.int32, (BAND, BAND), 0)
    col = lax.broadcasted_iota(jnp.int32, (BAND, BAND), 1)
    cur_ok = col <= row
    prev_ok = col >= row
    row2 = lax.broadcasted_iota(jnp.int32, (BAND, 2 * BAND), 0)
    col2 = lax.broadcasted_iota(jnp.int32, (BAND, 2 * BAND), 1)
    both_ok = ((col2 < BAND) & (col2 >= row2)) | ((col2 >= BAND) & (col2 - BAND <= row2))
    del prev_ok
    nt = (((1,), (1,)), ((), ()))

    def rows(start, dil):
        return pl.ds(start, BAND) if dil == 1 else pl.ds(start, BAND, stride=dil)

    for gi, (win, dil) in enumerate(DIL_GROUPS):
        assert win // dil == BAND
        sub_len = seq // dil
        for r in range(dil):
            kp = vp = None
            for n in range(sub_len // BAND):
                sl = rows(r + dil * BAND * n, dil)
                qb = q_refs[gi][sl, :].astype(BF16)
                kc = k_ref[sl, :].astype(BF16)
                vc = v_ref[sl, :].astype(BF16)
                if kp is None:
                    keys, vals, ok = kc, vc, cur_ok
                else:
                    keys = jnp.concatenate([kp, kc], axis=0)
                    vals = jnp.concatenate([vp, vc], axis=0)
                    ok = both_ok
                s = jnp.where(ok, lax.dot_general(qb, keys, nt, preferred_element_type=F32), NEG_BIG)
                m = jnp.max(s, axis=-1, keepdims=True)
                p = jnp.exp(s - m)
                l = jnp.sum(p, axis=-1, keepdims=True)
                acc = jnp.dot(p.astype(BF16), vals, preferred_element_type=F32)
                og_ref[gi, sl, :] = acc / l
                lse_ref[gi, sl, :] = jnp.broadcast_to(m + jnp.log(l), (BAND, HEAD_DIM))
                kp, vp = kc, vc

    l0, l1, l2 = lse_ref[0], lse_ref[1], lse_ref[2]
    mx = jnp.maximum(jnp.maximum(l0, l1), l2)
    w0, w1, w2 = jnp.exp(l0 - mx), jnp.exp(l1 - mx), jnp.exp(l2 - mx)
    o_ref[...] = (w0 * og_ref[0] + w1 * og_ref[1] + w2 * og_ref[2]) / (w0 + w1 + w2)


def _band_attention(q, k, v, *, batch, seq, n_heads):
    blk = (seq, HEAD_DIM)
    qspec = lambda gi: pl.BlockSpec(blk, lambda b, h: (b, gi * n_heads + h))
    kv = pl.BlockSpec(blk, lambda b, h: (b, h))
    return pl.pallas_call(
        functools.partial(_band_attn_kernel, seq=seq),
        out_shape=jax.ShapeDtypeStruct((batch * seq, n_heads * HEAD_DIM), F32),
        grid=(batch, n_heads),
        in_specs=[qspec(0), qspec(1), qspec(2), kv, kv],
        out_specs=kv,
        scratch_shapes=[pltpu.VMEM((3, seq, HEAD_DIM), F32), pltpu.VMEM((3, seq, HEAD_DIM), F32)],
        compiler_params=_params(("parallel", "parallel"), 40),
        name="band_attention",
    )(q, q, q, k, v)


def _decode_rows(w_buf, dec_seq):
    max_dil = max(d for _, d in DIL_GROUPS)
    dense_from = w_buf
    for win, dil in DIL_GROUPS:
        if dil < max_dil:
            dense_from = min(dense_from, w_buf - win)
    dense_from = max((dense_from // max_dil) * max_dil, 0)
    return max_dil, dense_from


def _decode_bias(w_buf, dec_seq, key_pos, n_pad):
    bias = np.full((len(DIL_GROUPS) * dec_seq, n_pad), NEG_BIG, np.float32)
    for gi, (win, dil) in enumerate(DIL_GROUPS):
        for j in range(dec_seq):
            dist = (w_buf + j) - key_pos
            ok = (dist >= 0) & (dist <= win) & (dist % dil == 0)
            bias[gi * dec_seq + j, :len(key_pos)][ok] = 0.0
    return bias


def _decode_attn_kernel(q_ref, kn_ref, vn_ref, ka_ref, kb_ref, va_ref, vb_ref, bias_ref, o_ref, *,
                        n_heads, dec_seq, n_groups, n_pad):
    bias = bias_ref[...]
    nt = (((1,), (1,)), ((), ()))
    na = ka_ref.shape[0] * (ka_ref.shape[1] // n_heads)
    nb = kb_ref.shape[0] * (kb_ref.shape[1] // n_heads)
    tail = n_pad - na - nb - dec_seq

    def head_rows(a_ref, b_ref, n_ref, h):
        ra = a_ref.shape[1] // n_heads
        rb = b_ref.shape[1] // n_heads
        xa = a_ref[:, pl.ds(h, ra, stride=n_heads), :].reshape(na, HEAD_DIM)
        xb = b_ref[:, pl.ds(h, rb, stride=n_heads), :].reshape(nb, HEAD_DIM)
        xn = n_ref[:, h * HEAD_DIM:(h + 1) * HEAD_DIM]
        parts = [xa, xb, xn]
        if tail:
            parts.append(jnp.zeros((tail, HEAD_DIM), F32))
        return jnp.concatenate(parts, axis=0).astype(BF16)

    outs = []
    for h in range(n_heads):
        kh = head_rows(ka_ref, kb_ref, kn_ref, h)
        vh = head_rows(va_ref, vb_ref, vn_ref, h)
        qh = jnp.concatenate(
            [q_ref[:, (gi * n_heads + h) * HEAD_DIM:(gi * n_heads + h + 1) * HEAD_DIM] for gi in range(n_groups)],
            axis=0).astype(BF16)
        s = lax.dot_general(qh, kh, nt, preferred_element_type=F32) + bias
        m = jnp.max(s, axis=-1, keepdims=True)
        p = jnp.exp(s - m)
        l = jnp.sum(p, axis=-1, keepdims=True)
        og = jnp.dot(p.astype(BF16), vh, preferred_element_type=F32) / l
        lse = m + jnp.log(l)
        ls = [lse[gi * dec_seq:(gi + 1) * dec_seq] for gi in range(n_groups)]
        mx = functools.reduce(jnp.maximum, ls)
        ws = [jnp.exp(x - mx) for x in ls]
        num = sum(w * og[gi * dec_seq:(gi + 1) * dec_seq] for gi, w in enumerate(ws))
        outs.append(num / sum(ws))
    o_ref[...] = jnp.concatenate(outs, axis=1)


def _decode_attention(q, k_new, v_new, cache_k, cache_v, *, q_row0, n_heads):
    batch, w_buf = cache_k.shape[:2]
    dec_seq = k_new.shape[0] // batch
    n_groups = len(DIL_GROUPS)
    comb, dense_from = _decode_rows(w_buf, dec_seq)
    assert dec_seq <= comb and comb % dec_seq == 0 and dec_seq % 8 == 0 and w_buf % comb == 0
    n_comb = dense_from // comb
    n_dense = (w_buf - dense_from) // comb
    key_pos = np.concatenate([
        (np.arange(n_comb)[:, None] * comb + np.arange(dec_seq)[None, :]).reshape(-1),
        dense_from + np.arange(w_buf - dense_from),
        w_buf + np.arange(dec_seq)])
    n_pad = -(-len(key_pos) // 128) * 128
    bias = jnp.asarray(_decode_bias(w_buf, dec_seq, key_pos, n_pad))
    ck = cache_k.reshape(batch, w_buf // comb, comb * n_heads, HEAD_DIM)
    cv = cache_v.reshape(batch, w_buf // comb, comb * n_heads, HEAD_DIM)
    tok = lambda width, row0=0: pl.BlockSpec((dec_seq, width), lambda b: (row0 // dec_seq + b, 0))
    comb_spec = pl.BlockSpec((None, n_comb, dec_seq * n_heads, HEAD_DIM), lambda b: (b, 0, 0, 0))
    dense_spec = pl.BlockSpec((None, n_dense, comb * n_heads, HEAD_DIM), lambda b: (b, n_comb // n_dense, 0, 0))
    assert n_comb % n_dense == 0
    return pl.pallas_call(
        functools.partial(_decode_attn_kernel, n_heads=n_heads, dec_seq=dec_seq, n_groups=n_groups, n_pad=n_pad),
        out_shape=jax.ShapeDtypeStruct((batch * dec_seq, n_heads * HEAD_DIM), F32),
        grid=(batch,),
        in_specs=[tok(q.shape[1], q_row0), tok(k_new.shape[1]), tok(v_new.shape[1]),
                  comb_spec, dense_spec, comb_spec, dense_spec,
                  pl.BlockSpec(bias.shape, lambda b: (0, 0))],
        out_specs=pl.BlockSpec((dec_seq, n_heads * HEAD_DIM), lambda b: (b, 0)),
        compiler_params=_params(("parallel",), 48),
        name="decode_attention",
    )(q, k_new, v_new, ck, ck, cv, cv, bias)


def _moe_and_ple(h, p_emb, layer, w, g_final, *, tm_moe, split=None):
    n, d = h.shape
    lanes = 128
    w_router = jnp.zeros((d, lanes), F32)
    w_router = w_router.at[:, :N_EXPERT_GROUPS].set(w["moe_w_rg"][layer])
    w_router = w_router.at[:, N_EXPERT_GROUPS:N_EXPERT_GROUPS + N_EXPERTS].set(w["moe_w_re"][layer])
    xn, info = _router(h, w["norm_ffn"][layer][None, :], w_router, tm=512)
    plan, pos0, pos1 = _route(info, tm=tm_moe, slab=d // SLAB_WIDTH)
    y_sorted = _experts(xn, plan, w["moe_w_gu"], w["moe_w_dn"], layer=layer, tm=tm_moe)
    ple = functools.partial(_ple, h, info, y_sorted, p_emb=p_emb, g_ple=w["norm_ple"][layer][None, :],
                            w_gate=w["ple_w_gate"][layer].astype(BF16), w_proj=w["ple_w_proj"][layer].astype(BF16),
                            g_final=g_final, tm=256)
    if split is None:
        return ple(pos0, pos1)
    return (ple(pos0[:split], pos1[:split], row0=0, n_rows=split),
            ple(pos0[split:], pos1[split:], row0=split, n_rows=n - split))


def kernel(x_prompt, x_sample, p_prompt, p_sample, state_gla, cache_k, cache_v, norm_mix, norm_ffn, norm_ple,
           norm_kv, norm_final, gla_w_in, gla_w_a2, gla_b_a, gla_g_out, gla_w_out, w_kv, dil_w_q, dil_w_out,
           moe_w_rg, moe_w_re, moe_w_gu, moe_w_dn, ple_w_gate, ple_w_proj):
    w = dict(norm_ffn=norm_ffn, norm_ple=norm_ple, moe_w_rg=moe_w_rg, moe_w_re=moe_w_re, moe_w_gu=moe_w_gu,
             moe_w_dn=moe_w_dn, ple_w_gate=ple_w_gate, ple_w_proj=ple_w_proj)
    bp, tp, d = x_prompt.shape
    bs, ts, _ = x_sample.shape
    n_p, n_s = bp * tp, bs * ts
    depth = p_prompt.shape[0]
    assert depth == 2 and state_gla.shape[0] == 1
    heads, dk, dv = state_gla.shape[2:]
    n_kv = cache_k.shape[2]
    past_len = cache_k.shape[1]
    qk_w, v_w = heads * dk, heads * dv
    lowrank = gla_w_a2.shape[1]

    x = jnp.concatenate([x_prompt.reshape(n_p, d), x_sample.reshape(n_s, d)], axis=0)
    p_emb = jnp.concatenate([p_prompt.reshape(depth, n_p, -1), p_sample.reshape(depth, n_s, -1)], axis=1)

    half = HEAD_DIM // 2
    inv = ROPE_THETA ** (-jnp.arange(half, dtype=F32) / half)
    pos = jnp.concatenate([jnp.tile(jnp.arange(tp), bp), jnp.tile(past_len + jnp.arange(ts), bs)]).astype(F32)
    ang = pos[:, None] * inv[None, :]
    rope = (jnp.concatenate([jnp.cos(ang), jnp.cos(ang)], axis=1),
            jnp.concatenate([-jnp.sin(ang), jnp.sin(ang)], axis=1))

    g_mix0 = norm_mix[0][None, :]
    w_in = gla_w_in[0].astype(BF16)
    main_w = 2 * qk_w + 2 * v_w
    tm_wide = 1024 if (n_p + n_s) % 1024 == 0 else 512
    z = _norm_linear(x, g_mix0, w_in, col_start=0, n_cols=main_w, tn=1024, tm=tm_wide)
    w_a1 = jnp.zeros((d, 128), BF16).at[:, :lowrank].set(w_in[:, main_w:])
    w_a2 = jnp.zeros((128, qk_w), F32).at[:lowrank].set(gla_w_a2[0])
    g = _gla_gate(x, g_mix0, w_a1, w_a2, gla_b_a[0][None, :], tm=512)
    c_p = int(np.gcd(tp, GLA_CHUNK))
    c_s = int(np.gcd(ts, GLA_CHUNK))
    gla = functools.partial(_gla_scan, z, g, heads=heads, dk=dk, dv=dv)
    o_p, sg_p = gla(None, row0=0, batch=bp, seq=tp, c=c_p, lanes=2 if bp % 2 == 0 else 1)
    o_s, sg_s = gla(state_gla[0], row0=n_p, batch=bs, seq=ts, c=c_s, lanes=2 if bs % 2 == 0 else 1)
    h = _gla_out(o_p, o_s, z, x, gla_g_out[0][None, :], gla_w_out[0].astype(BF16), heads=heads, tm=256)
    h = _moe_and_ple(h, p_emb[0], 0, w, None, tm_moe=128)

    w_kv_b = w_kv.astype(BF16)
    kv_w = n_kv * HEAD_DIM
    g_kv = norm_kv[None, :]
    kv_lin = functools.partial(_norm_linear, h, g_kv, w_kv_b, n_cols=kv_w, tn=kv_w, tm=512)
    k_p = kv_lin(col_start=0, rope=rope, row0=0, n_rows=n_p)
    k_s = kv_lin(col_start=0, rope=rope, row0=n_p, n_rows=n_s)
    v_p = kv_lin(col_start=kv_w, row0=0, n_rows=n_p)
    v_s = kv_lin(col_start=kv_w, row0=n_p, n_rows=n_s)
    q_all = _norm_linear(h, norm_mix[1][None, :], dil_w_q[0].astype(BF16), col_start=0,
                         n_cols=dil_w_q.shape[2], tn=kv_w, tm=tm_wide, rope=rope, scale=HEAD_DIM ** -0.5)

    a_p = _band_attention(q_all, k_p, v_p, batch=bp, seq=tp, n_heads=n_kv)
    a_s = _decode_attention(q_all, k_s, v_s, cache_k, cache_v, q_row0=n_p, n_heads=n_kv)
    h = _linear_res(a_p, a_s, dil_w_out[0].astype(BF16), h, tm=512)
    y_p, y_s = _moe_and_ple(h, p_emb[1], 1, w, norm_final[None, :], tm_moe=128, split=n_p)

    rows = min(past_len, tp)
    as_heads = lambda a, b, t: a.reshape(b, t, n_kv, HEAD_DIM)
    return (y_p.reshape(bp, tp, d), y_s.reshape(bs, ts, d), sg_p[None], sg_s[None],
            as_heads(k_p, bp, tp)[:, tp - rows:], as_heads(v_p, bp, tp)[:, tp - rows:],
            as_heads(k_s, bs, ts), as_heads(v_s, bs, ts))
```

```python
import functools

import numpy as np
import jax
import jax.numpy as jnp
from jax import lax
from jax.experimental import pallas as pl
from jax.experimental.pallas import tpu as pltpu

F32 = jnp.float32
BF16 = jnp.bfloat16
HIGHEST = lax.Precision.HIGHEST

NORM_EPS = 1e-6
GLA_TAU = 16.0
GLA_CHUNK = 64
GLA_PAD = 128
HEAD_DIM = 128
BAND = 128
DIL_GROUPS = ((128, 1), (512, 4), (2048, 16))
ROPE_THETA = 10000.0
N_EXPERT_GROUPS = 4
EXPERTS_PER_GROUP = 8
N_EXPERTS = N_EXPERT_GROUPS * EXPERTS_PER_GROUP
NEG_BIG = -1e30
MIB = 1024 * 1024


def _params(semantics, vmem_mib):
    return pltpu.CompilerParams(dimension_semantics=semantics, vmem_limit_bytes=vmem_mib * MIB)


def _rms(x, gain):
    var = jnp.mean(x * x, axis=-1, keepdims=True)
    return x * lax.rsqrt(var + NORM_EPS) * gain


def _sigmoid(x):
    return 1.0 / (1.0 + jnp.exp(-x))


def _norm_linear_kernel(x_ref, g_ref, w_ref, *rest, rope, scale):
    if rope:
        cos_ref, sin_ref, o_ref, xn_ref = rest
    else:
        o_ref, xn_ref = rest

    @pl.when(pl.program_id(1) == 0)
    def _():
        xn_ref[...] = _rms(x_ref[...], g_ref[...]).astype(BF16)

    acc = jnp.dot(xn_ref[...], w_ref[...], preferred_element_type=F32)
    if rope:
        cos = cos_ref[...]
        sin = sin_ref[...]
        parts = []
        for c in range(acc.shape[1] // HEAD_DIM):
            y = acc[:, c * HEAD_DIM:(c + 1) * HEAD_DIM]
            parts.append(y * cos + pltpu.roll(y, HEAD_DIM // 2, 1) * sin)
        acc = parts[0] if len(parts) == 1 else jnp.concatenate(parts, axis=1)
    if scale != 1.0:
        acc = acc * scale
    o_ref[...] = acc


def _norm_linear(x, gain, w, *, col_start, n_cols, tn, tm, rope=None, scale=1.0, row0=0, n_rows=None):
    k = x.shape[1]
    n = x.shape[0] if n_rows is None else n_rows
    grid = (n // tm, n_cols // tn)
    c0 = col_start // tn
    r0 = row0 // tm
    in_specs = [
        pl.BlockSpec((tm, k), lambda i, j: (r0 + i, 0)),
        pl.BlockSpec((1, k), lambda i, j: (0, 0)),
        pl.BlockSpec((k, tn), lambda i, j: (0, c0 + j)),
    ]
    args = [x, gain, w]
    if rope is not None:
        in_specs += [pl.BlockSpec((tm, HEAD_DIM), lambda i, j: (r0 + i, 0))] * 2
        args += list(rope)
    return pl.pallas_call(
        functools.partial(_norm_linear_kernel, rope=rope is not None, scale=scale),
        out_shape=jax.ShapeDtypeStruct((n, n_cols), F32),
        grid=grid,
        in_specs=in_specs,
        out_specs=pl.BlockSpec((tm, tn), lambda i, j: (i, j)),
        scratch_shapes=[pltpu.VMEM((tm, k), BF16)],
        compiler_params=_params(("parallel", "arbitrary"), 48),
        name="norm_linear",
    )(*args)


def _gla_gate_kernel(x_ref, gn_ref, w1_ref, w2_ref, b_ref, o_ref):
    xn = _rms(x_ref[...], gn_ref[...]).astype(BF16)
    a1 = jnp.dot(xn, w1_ref[...], preferred_element_type=F32)
    pre = jnp.dot(a1, w2_ref[...], precision=HIGHEST, preferred_element_type=F32) + b_ref[...]
    o_ref[...] = (jnp.minimum(pre, 0.0) - jnp.log1p(jnp.exp(-jnp.abs(pre)))) * (1.0 / GLA_TAU)


def _gla_gate(x, gain, w1, w2, b_a, *, tm):
    n, k = x.shape
    r, qk = w2.shape
    return pl.pallas_call(
        _gla_gate_kernel,
        out_shape=jax.ShapeDtypeStruct((n, qk), F32),
        grid=(n // tm,),
        in_specs=[
            pl.BlockSpec((tm, k), lambda i: (i, 0)),
            pl.BlockSpec((1, k), lambda i: (0, 0)),
            pl.BlockSpec((k, r), lambda i: (0, 0)),
            pl.BlockSpec((r, qk), lambda i: (0, 0)),
            pl.BlockSpec((1, qk), lambda i: (0, 0)),
        ],
        out_specs=pl.BlockSpec((tm, qk), lambda i: (i, 0)),
        compiler_params=_params(("parallel",), 32),
        name="gla_gate",
    )(x, gain, w1, w2, b_a)


def _gla_kernel(*refs, c, heads, lanes, has_s0, q_scale):
    lane_refs = [refs[4 * i:4 * i + 4] for i in range(lanes)]
    s0_ref = refs[4 * lanes] if has_s0 else None
    o_ref, so_ref, s_scr = refs[-3:]
    n = pl.program_id(1)
    dk = lane_refs[0][0].shape[1] // heads
    dv = lane_refs[0][2].shape[1] // heads

    @pl.when(n == 0)
    def _():
        if has_s0:
            s_scr[...] = s0_ref[...]
        else:
            s_scr[...] = jnp.zeros_like(s_scr)

    def pad(a):
        if c == GLA_PAD:
            return a
        return jnp.concatenate([a, jnp.zeros((GLA_PAD - c, a.shape[1]), a.dtype)], axis=0)

    row = lax.broadcasted_iota(jnp.int32, (GLA_PAD, GLA_PAD), 0)
    col = lax.broadcasted_iota(jnp.int32, (GLA_PAD, GLA_PAD), 1)
    tri = jnp.where(row >= col, 1.0, 0.0).astype(F32)
    rc = lax.broadcasted_iota(jnp.int32, (c, c), 0)
    cc = lax.broadcasted_iota(jnp.int32, (c, c), 1)
    for lane, (q_ref, k_ref, v_ref, g_ref) in enumerate(lane_refs):
        bp = jnp.dot(tri, pad(g_ref[...]), precision=HIGHEST, preferred_element_type=F32)
        b = bp[:c]
        k = k_ref[...]
        v = v_ref[...]
        qe = (q_ref[...] * q_scale * jnp.exp(b)).astype(BF16)
        ke = (k * jnp.exp(-b)).astype(BF16)
        vb = v.astype(BF16)
        k_t = pad(k).T
        b_t = bp.T
        b_last = b_t[:, c - 1:c]
        kd_t = (k_t * jnp.exp(b_last - b_t)).astype(BF16)
        decay = jnp.exp(b_last)
        vp = pad(v).astype(BF16)
        for h in range(heads):
            ks = slice(h * dk, (h + 1) * dk)
            vs = slice(h * dv, (h + 1) * dv)
            s_old = s_scr[lane, h]
            o = jnp.dot(qe[:, ks], s_old.astype(BF16), preferred_element_type=F32)
            a = lax.dot_general(qe[:, ks], ke[:, ks], (((1,), (1,)), ((), ())), preferred_element_type=F32)
            a = jnp.where(rc >= cc, a, 0.0)
            o_ref[lane, :, vs] = o + jnp.dot(a.astype(BF16), vb[:, vs], preferred_element_type=F32)
            s_scr[lane, h] = decay[ks] * s_old + jnp.dot(kd_t[ks], vp[:, vs], preferred_element_type=F32)

    @pl.when(n == pl.num_programs(1) - 1)
    def _():
        so_ref[...] = s_scr[...]


def _gla_scan(z, g, s0, *, row0, batch, seq, heads, dk, dv, c, lanes):
    nchunk = seq // c
    rb0 = row0 // c
    qk_w, v_w = heads * dk, heads * dv
    in_specs, args = [], []
    for lane in range(lanes):
        rows = lambda b, n, lane=lane: rb0 + (b * lanes + lane) * nchunk + n
        in_specs += [
            pl.BlockSpec((c, qk_w), lambda b, n, rows=rows: (rows(b, n), 0)),
            pl.BlockSpec((c, qk_w), lambda b, n, rows=rows: (rows(b, n), 1)),
            pl.BlockSpec((c, v_w), lambda b, n, rows=rows: (rows(b, n), 2 * qk_w // v_w)),
            pl.BlockSpec((c, qk_w), lambda b, n, rows=rows: (rows(b, n), 0)),
        ]
        args += [z, z, z, g]
    state_spec = pl.BlockSpec((lanes, heads, dk, dv), lambda b, n: (b, 0, 0, 0))
    if s0 is not None:
        in_specs.append(state_spec)
        args.append(s0)
    o, s_out = pl.pallas_call(
        functools.partial(_gla_kernel, c=c, heads=heads, lanes=lanes, has_s0=s0 is not None,
                          q_scale=float(dk) ** -0.5),
        out_shape=(jax.ShapeDtypeStruct((batch, seq, v_w), F32),
                   jax.ShapeDtypeStruct((batch, heads, dk, dv), F32)),
        grid=(batch // lanes, nchunk),
        in_specs=in_specs,
        out_specs=(pl.BlockSpec((lanes, c, v_w), lambda b, n: (b, n, 0)), state_spec),
        scratch_shapes=[pltpu.VMEM((lanes, heads, dk, dv), F32)],
        compiler_params=_params(("parallel", "arbitrary"), 48),
        name="gla_scan",
    )(*args)
    return o.reshape(batch * seq, v_w), s_out


def _two_group_specs(tm, width, n_first):
    t_first = n_first // tm
    return (pl.BlockSpec((tm, width), lambda i: (jnp.minimum(i, t_first - 1), 0)),
            pl.BlockSpec((tm, width), lambda i: (jnp.maximum(i - t_first, 0), 0)))


def _pick_group(first_ref, second_ref, n_first):
    in_first = pl.program_id(0) < n_first // first_ref.shape[0]
    return jnp.where(in_first, first_ref[...], second_ref[...])


def _gla_out_kernel(op_ref, os_ref, r_ref, x_ref, go_ref, w_ref, h_ref, *, heads, n_first):
    o = _pick_group(op_ref, os_ref, n_first)
    dv = o.shape[1] // heads
    go = go_ref[...]
    parts = [_rms(o[:, h * dv:(h + 1) * dv], go) for h in range(heads)]
    on = jnp.concatenate(parts, axis=1)
    r = r_ref[...]
    y = (on * (r * _sigmoid(r))).astype(BF16)
    h_ref[...] = x_ref[...] + jnp.dot(y, w_ref[...], preferred_element_type=F32)


def _gla_out(o_p, o_s, z, x, g_out, w_out, *, heads, tm):
    n, d = x.shape
    vw = o_p.shape[1]
    n_first = o_p.shape[0]
    rblk = (z.shape[1] - vw) // vw
    return pl.pallas_call(
        functools.partial(_gla_out_kernel, heads=heads, n_first=n_first),
        out_shape=jax.ShapeDtypeStruct((n, d), F32),
        grid=(n // tm,),
        in_specs=[
            *_two_group_specs(tm, vw, n_first),
            pl.BlockSpec((tm, vw), lambda i: (i, rblk)),
            pl.BlockSpec((tm, d), lambda i: (i, 0)),
            pl.BlockSpec((1, vw // heads), lambda i: (0, 0)),
            pl.BlockSpec((vw, d), lambda i: (0, 0)),
        ],
        out_specs=pl.BlockSpec((tm, d), lambda i: (i, 0)),
        compiler_params=_params(("parallel",), 44),
        name="gla_out",
    )(o_p, o_s, z, x, g_out, w_out)


SLAB_WIDTH = 128
SLAB_DTYPE = F32


def _from_slabs(ref, rows):
    slab = ref.shape[0] // rows
    return jnp.concatenate([ref[pl.ds(s, rows, stride=slab), :] for s in range(slab)], axis=1)


def _to_slabs(ref, val):
    rows = val.shape[0]
    slab = ref.shape[0] // rows
    for s in range(slab):
        ref[pl.ds(s, rows, stride=slab), :] = val[:, s * 128:(s + 1) * 128]


def _router_kernel(x_ref, g_ref, w_ref, xn_ref, info_ref):
    xn = _rms(x_ref[...], g_ref[...])
    _to_slabs(xn_ref, xn)
    lg = jnp.dot(xn, w_ref[...], precision=HIGHEST, preferred_element_type=F32)
    lane = lax.broadcasted_iota(jnp.int32, lg.shape, 1).astype(F32)
    far = float(lg.shape[1])
    is_grp = lane < N_EXPERT_GROUPS
    lgm = jnp.where(is_grp, lg, NEG_BIG)
    gmax = jnp.max(lgm, axis=-1, keepdims=True)
    gsum = jnp.sum(jnp.where(is_grp, jnp.exp(lg - gmax), 0.0), axis=-1, keepdims=True)
    p_top = 1.0 / gsum
    g_top = jnp.min(jnp.where(lgm == gmax, lane, far), axis=-1, keepdims=True)
    lo = N_EXPERT_GROUPS + g_top * EXPERTS_PER_GROUP
    in_grp = (lane >= lo) & (lane < lo + EXPERTS_PER_GROUP)
    le = jnp.where(in_grp, lg, NEG_BIG)
    m0 = jnp.max(le, axis=-1, keepdims=True)
    i0 = jnp.min(jnp.where(le == m0, lane, far), axis=-1, keepdims=True)
    le1 = jnp.where(lane == i0, NEG_BIG, le)
    m1 = jnp.max(le1, axis=-1, keepdims=True)
    i1 = jnp.min(jnp.where(le1 == m1, lane, far), axis=-1, keepdims=True)
    t = jnp.exp(m1 - m0)
    w0 = p_top / (1.0 + t)
    w1 = p_top * t / (1.0 + t)
    info = jnp.where(lane == 0.0, i0 - N_EXPERT_GROUPS,
                     jnp.where(lane == 1.0, i1 - N_EXPERT_GROUPS,
                               jnp.where(lane == 2.0, w0, jnp.where(lane == 3.0, w1, 0.0))))
    info_ref[...] = info


def _router(x, gain, w_router, *, tm):
    n, k = x.shape
    lanes = w_router.shape[1]
    slab = k // SLAB_WIDTH
    return pl.pallas_call(
        _router_kernel,
        out_shape=(jax.ShapeDtypeStruct((n * slab, 128), SLAB_DTYPE), jax.ShapeDtypeStruct((n, lanes), F32)),
        grid=(n // tm,),
        in_specs=[
            pl.BlockSpec((tm, k), lambda i: (i, 0)),
            pl.BlockSpec((1, k), lambda i: (0, 0)),
            pl.BlockSpec((k, lanes), lambda i: (0, 0)),
        ],
        out_specs=(pl.BlockSpec((tm * slab, 128), lambda i: (i, 0)), pl.BlockSpec((tm, lanes), lambda i: (i, 0))),
        compiler_params=_params(("parallel",), 32),
        name="moe_router",
    )(x, gain, w_router)


def _slab_copy(src_hbm, src_row, dst, dst_row, slab, sem):
    src_row = pl.multiple_of(src_row, slab)
    dst_row = pl.multiple_of(dst_row, slab)
    return pltpu.make_async_copy(src_hbm.at[pl.ds(src_row, slab)], dst.at[pl.ds(dst_row, slab)], sem)


def _start_slab_gather(src_hbm, idx_ref, base, dst, slab, sem):
    def body(r, carry):
        _slab_copy(src_hbm, idx_ref[base + r], dst, r * slab, slab, sem).start()
        return carry
    lax.fori_loop(0, dst.shape[0] // slab, body, 0, unroll=8)


def _wait_slab_gather(src_hbm, dst, slab, sem):
    def body(r, carry):
        _slab_copy(src_hbm, 0, dst, r * slab, slab, sem).wait()
        return carry
    lax.fori_loop(0, dst.shape[0] // slab, body, 0, unroll=8)


def _expert_weight_copies(wgu_hbm, wdn_hbm, layer, expert, wgu_f32, wdn_f32, wslot, wsem):
    return (pltpu.make_async_copy(wgu_hbm.at[layer, expert], wgu_f32.at[wslot], wsem.at[wslot, 0]),
            pltpu.make_async_copy(wdn_hbm.at[layer, expert], wdn_f32.at[wslot], wsem.at[wslot, 1]))


def _experts_kernel(te_ref, nu_ref, rt_ref, first_ref, wslot_ref, next_ref, x_hbm, wgu_hbm, wdn_hbm, o_ref,
                    xbuf, sem, wgu_f32, wdn_f32, wsem, wgu_bf, wdn_bf, *, layer):
    t = pl.program_id(0)
    slab = wgu_bf.shape[0] // SLAB_WIDTH
    tm = xbuf.shape[1] // slab
    n_used = nu_ref[0]
    slot = lax.rem(t, 2)
    weights = functools.partial(_expert_weight_copies, wgu_hbm, wdn_hbm, layer,
                                wgu_f32=wgu_f32, wdn_f32=wdn_f32, wsem=wsem)

    @pl.when((t == 0) & (n_used > 0))
    def _():
        _start_slab_gather(x_hbm, rt_ref, 0, xbuf.at[0], slab, sem.at[0])
        for c in weights(te_ref[0], wslot=0):
            c.start()

    @pl.when(t + 1 < n_used)
    def _():
        _start_slab_gather(x_hbm, rt_ref, (t + 1) * tm, xbuf.at[1 - slot], slab, sem.at[1 - slot])

    @pl.when(t < n_used)
    def _():
        @pl.when(first_ref[t] == 1)
        def _():
            ws = wslot_ref[t]

            @pl.when(next_ref[t] >= 0)
            def _():
                for c in weights(next_ref[t], wslot=1 - ws):
                    c.start()

            for c in weights(te_ref[t], wslot=ws):
                c.wait()
            wgu_bf[...] = wgu_f32[ws].astype(BF16)
            wdn_bf[...] = wdn_f32[ws].astype(BF16)

        _wait_slab_gather(x_hbm, xbuf.at[slot], slab, sem.at[slot])
        x = _from_slabs(xbuf.at[slot], tm).astype(BF16)
        hg = jnp.dot(x, wgu_bf[...], preferred_element_type=F32)
        f = hg.shape[1] // 2
        a = hg[:, :f]
        hact = (a * _sigmoid(a)) * hg[:, f:]
        _to_slabs(o_ref, jnp.dot(hact.astype(BF16), wdn_bf[...], preferred_element_type=F32))

    @pl.when(t >= n_used)
    def _():
        o_ref[...] = jnp.zeros_like(o_ref)


def _experts(xn_slabs, plan, w_gu, w_dn, *, layer, tm):
    p = plan[2].shape[0]
    d, f2 = w_gu.shape[-2:]
    f = w_dn.shape[-2]
    slab = d // SLAB_WIDTH
    any_space = pl.BlockSpec(memory_space=pl.ANY)
    grid_spec = pltpu.PrefetchScalarGridSpec(
        num_scalar_prefetch=len(plan),
        grid=(p // tm,),
        in_specs=[any_space, any_space, any_space],
        out_specs=pl.BlockSpec((tm * slab, 128), lambda t, *_: (t, 0)),
        scratch_shapes=[pltpu.VMEM((2, tm * slab, 128), SLAB_DTYPE), pltpu.SemaphoreType.DMA((2,)),
                        pltpu.VMEM((2, d, f2), F32), pltpu.VMEM((2, f, d), F32), pltpu.SemaphoreType.DMA((2, 2)),
                        pltpu.VMEM((d, f2), BF16), pltpu.VMEM((f, d), BF16)],
    )
    return pl.pallas_call(
        functools.partial(_experts_kernel, layer=layer),
        out_shape=jax.ShapeDtypeStruct((p * slab, 128), SLAB_DTYPE),
        grid_spec=grid_spec,
        compiler_params=_params(("arbitrary",), 56),
        name="moe_experts",
    )(*plan, xn_slabs, w_gu, w_dn)


def _route(info, *, tm, slab):
    n = info.shape[0]
    e = info[:, :2].astype(jnp.int32).reshape(-1)
    onehot = (e[:, None] == jnp.arange(N_EXPERTS, dtype=jnp.int32)[None, :]).astype(jnp.int32)
    before = jnp.cumsum(onehot, axis=0) - onehot
    counts = jnp.sum(onehot, axis=0)
    padded = ((counts + tm - 1) // tm) * tm
    ends = jnp.cumsum(padded)
    starts = ends - padded
    pos = jnp.sum(onehot * (starts[None, :] + before), axis=1)
    p = ((2 * n + N_EXPERTS * (tm - 1)) // tm + 1) * tm
    row_start = jnp.zeros((p,), jnp.int32).at[pos].set((jnp.arange(2 * n, dtype=jnp.int32) // 2) * slab,
                                                       unique_indices=True, indices_are_sorted=False)
    tile_start = jnp.arange(p // tm, dtype=jnp.int32) * tm
    n_used = (ends[-1] // tm).astype(jnp.int32)
    used = tile_start < ends[-1]
    te = jnp.sum((tile_start[:, None] >= ends[None, :]).astype(jnp.int32), axis=1)
    last = jnp.sum((jnp.maximum(ends[-1] - 1, 0) >= ends).astype(jnp.int32))
    te = jnp.where(used, te, last).astype(jnp.int32)
    first = used & (te != jnp.concatenate([jnp.full((1,), -1, jnp.int32), te[:-1]]))
    wslot = lax.rem(jnp.cumsum(first.astype(jnp.int32)) - 1, 2)
    later = used[None, :] & (te[None, :] > te[:, None])
    nxt = jnp.min(jnp.where(later, te[None, :], N_EXPERTS), axis=1)
    nxt = jnp.where(nxt < N_EXPERTS, nxt, -1)
    plan = (te, n_used.reshape(1), row_start, first.astype(jnp.int32), wslot.astype(jnp.int32),
            nxt.astype(jnp.int32))
    pos2 = (pos * slab).reshape(n, 2)
    return plan, pos2[:, 0], pos2[:, 1]


def _ple_kernel(pos0_ref, pos1_ref, h_ref, info_ref, y_hbm, p_ref, gp_ref, wg_ref, wp_ref, *rest, final):
    if final:
        gf_ref, o_ref, ybuf, sem = rest
    else:
        o_ref, ybuf, sem = rest
    i = pl.program_id(0)
    tm, d = h_ref.shape
    slab = d // SLAB_WIDTH
    slot = lax.rem(i, 2)

    def start(tile, s):
        _start_slab_gather(y_hbm, pos0_ref, tile * tm, ybuf.at[s, 0], slab, sem.at[s])
        _start_slab_gather(y_hbm, pos1_ref, tile * tm, ybuf.at[s, 1], slab, sem.at[s])

    @pl.when(i == 0)
    def _():
        start(0, 0)

    @pl.when(i + 1 < pl.num_programs(0))
    def _():
        start(i + 1, 1 - slot)

    _wait_slab_gather(y_hbm, ybuf.at[slot, 0], slab, sem.at[slot])
    _wait_slab_gather(y_hbm, ybuf.at[slot, 1], slab, sem.at[slot])
    info = info_ref[...]
    y0 = _from_slabs(ybuf.at[slot, 0], tm)
    y1 = _from_slabs(ybuf.at[slot, 1], tm)
    h = h_ref[...] + (info[:, 2:3] * y0 + info[:, 3:4] * y1)
    hn = _rms(h, gp_ref[...]).astype(BF16)
    gate = _sigmoid(jnp.dot(hn, wg_ref[...], preferred_element_type=F32))
    proj = jnp.dot(p_ref[...].astype(BF16), wp_ref[...], preferred_element_type=F32)
    out = h + gate * proj
    if final:
        out = _rms(out, gf_ref[...])
    o_ref[...] = out


def _ple(h, info, y_sorted, pos0, pos1, *, p_emb, g_ple, w_gate, w_proj, g_final, tm, row0=0, n_rows=None):
    d = h.shape[1]
    n = h.shape[0] if n_rows is None else n_rows
    pd = p_emb.shape[1]
    r0 = row0 // tm
    row = lambda i, p0, p1: (r0 + i, 0)
    fix = lambda i, p0, p1: (0, 0)
    in_specs = [
        pl.BlockSpec((tm, d), row), pl.BlockSpec((tm, info.shape[1]), row), pl.BlockSpec(memory_space=pl.ANY),
        pl.BlockSpec((tm, pd), row), pl.BlockSpec((1, d), fix),
        pl.BlockSpec((d, d), fix), pl.BlockSpec((pd, d), fix),
    ]
    args = [h, info, y_sorted, p_emb, g_ple, w_gate, w_proj]
    if g_final is not None:
        in_specs.append(pl.BlockSpec((1, d), fix))
        args.append(g_final)
    grid_spec = pltpu.PrefetchScalarGridSpec(
        num_scalar_prefetch=2,
        grid=(n // tm,),
        in_specs=in_specs,
        out_specs=pl.BlockSpec((tm, d), lambda i, p0, p1: (i, 0)),
        scratch_shapes=[pltpu.VMEM((2, 2, tm * (d // SLAB_WIDTH), 128), SLAB_DTYPE), pltpu.SemaphoreType.DMA((2,))],
    )
    return pl.pallas_call(
        functools.partial(_ple_kernel, final=g_final is not None),
        out_shape=jax.ShapeDtypeStruct((n, d), F32),
        grid_spec=grid_spec,
        compiler_params=_params(("arbitrary",), 52),
        name="ple",
    )(pos0, pos1, *args)


def _linear_res_kernel(ap_ref, as_ref, w_ref, h_ref, o_ref, *, n_first):
    a = _pick_group(ap_ref, as_ref, n_first).astype(BF16)
    o_ref[...] = h_ref[...] + jnp.dot(a, w_ref[...], preferred_element_type=F32)


def _linear_res(a_p, a_s, w, h, *, tm):
    n, d = h.shape
    k = a_p.shape[1]
    n_first = a_p.shape[0]
    return pl.pallas_call(
        functools.partial(_linear_res_kernel, n_first=n_first),
        out_shape=jax.ShapeDtypeStruct((n, d), F32),
        grid=(n // tm,),
        in_specs=[*_two_group_specs(tm, k, n_first), pl.BlockSpec((k, d), lambda i: (0, 0)),
                  pl.BlockSpec((tm, d), lambda i: (i, 0))],
        out_specs=pl.BlockSpec((tm, d), lambda i: (i, 0)),
        compiler_params=_params(("parallel",), 40),
        name="attn_out",
    )(a_p, a_s, w, h)


def _band_attn_kernel(q0_ref, q1_ref, q2_ref, k_ref, v_ref, o_ref, og_ref, lse_ref, *, seq):
    q_refs = (q0_ref, q1_ref, q2_ref)
    row = lax.broadcasted_iota(jnp.int32, (BAND, BAND), 0)
    col = lax.broadcasted_iota(jnp.int32, (BAND, BAND), 1)
    cur_ok = col <= row
    prev_ok = col >= row
    row2 = lax.broadcasted_iota(jnp.int32, (BAND, 2 * BAND), 0)
    col2 = lax.broadcasted_iota(jnp.int32, (BAND, 2 * BAND), 1)
    both_ok = ((col2 < BAND) & (col2 >= row2)) | ((col2 >= BAND) & (col2 - BAND <= row2))
    del prev_ok
    nt = (((1,), (1,)), ((), ()))

    def rows(start, dil):
        return pl.ds(start, BAND) if dil == 1 else pl.ds(start, BAND, stride=dil)

    for gi, (win, dil) in enumerate(DIL_GROUPS):
        assert win // dil == BAND
        sub_len = seq // dil
        for r in range(dil):
            kp = vp = None
            for n in range(sub_len // BAND):
                sl = rows(r + dil * BAND * n, dil)
                qb = q_refs[gi][sl, :].astype(BF16)
                kc = k_ref[sl, :].astype(BF16)
                vc = v_ref[sl, :].astype(BF16)
                if kp is None:
                    keys, vals, ok = kc, vc, cur_ok
                else:
                    keys = jnp.concatenate([kp, kc], axis=0)
                    vals = jnp.concatenate([vp, vc], axis=0)
                    ok = both_ok
                s = jnp.where(ok, lax.dot_general(qb, keys, nt, preferred_element_type=F32), NEG_BIG)
                m = jnp.max(s, axis=-1, keepdims=True)
                p = jnp.exp(s - m)
                l = jnp.sum(p, axis=-1, keepdims=True)
                acc = jnp.dot(p.astype(BF16), vals, preferred_element_type=F32)
                og_ref[gi, sl, :] = acc / l
                lse_ref[gi, sl, :] = jnp.broadcast_to(m + jnp.log(l), (BAND, HEAD_DIM))
                kp, vp = kc, vc

    l0, l1, l2 = lse_ref[0], lse_ref[1], lse_ref[2]
    mx = jnp.maximum(jnp.maximum(l0, l1), l2)
    w0, w1, w2 = jnp.exp(l0 - mx), jnp.exp(l1 - mx), jnp.exp(l2 - mx)
    o_ref[...] = (w0 * og_ref[0] + w1 * og_ref[1] + w2 * og_ref[2]) / (w0 + w1 + w2)


def _band_attention(q, k, v, *, batch, seq, n_heads):
    blk = (seq, HEAD_DIM)
    qspec = lambda gi: pl.BlockSpec(blk, lambda b, h: (b, gi * n_heads + h))
    kv = pl.BlockSpec(blk, lambda b, h: (b, h))
    return pl.pallas_call(
        functools.partial(_band_attn_kernel, seq=seq),
        out_shape=jax.ShapeDtypeStruct((batch * seq, n_heads * HEAD_DIM), F32),
        grid=(batch, n_heads),
        in_specs=[qspec(0), qspec(1), qspec(2), kv, kv],
        out_specs=kv,
        scratch_shapes=[pltpu.VMEM((3, seq, HEAD_DIM), F32), pltpu.VMEM((3, seq, HEAD_DIM), F32)],
        compiler_params=_params(("parallel", "parallel"), 40),
        name="band_attention",
    )(q, q, q, k, v)


def _decode_rows(w_buf, dec_seq):
    max_dil = max(d for _, d in DIL_GROUPS)
    dense_from = w_buf
    for win, dil in DIL_GROUPS:
        if dil < max_dil:
            dense_from = min(dense_from, w_buf - win)
    dense_from = max((dense_from // max_dil) * max_dil, 0)
    return max_dil, dense_from


def _decode_bias(w_buf, dec_seq, key_pos, n_pad):
    bias = np.full((len(DIL_GROUPS) * dec_seq, n_pad), NEG_BIG, np.float32)
    for gi, (win, dil) in enumerate(DIL_GROUPS):
        for j in range(dec_seq):
            dist = (w_buf + j) - key_pos
            ok = (dist >= 0) & (dist <= win) & (dist % dil == 0)
            bias[gi * dec_seq + j, :len(key_pos)][ok] = 0.0
    return bias


def _decode_attn_kernel(q_ref, kn_ref, vn_ref, ka_ref, kb_ref, va_ref, vb_ref, bias_ref, o_ref, *,
                        n_heads, dec_seq, n_groups, n_pad):
    bias = bias_ref[...]
    nt = (((1,), (1,)), ((), ()))
    na = ka_ref.shape[0] * (ka_ref.shape[1] // n_heads)
    nb = kb_ref.shape[0] * (kb_ref.shape[1] // n_heads)
    tail = n_pad - na - nb - dec_seq

    def head_rows(a_ref, b_ref, n_ref, h):
        ra = a_ref.shape[1] // n_heads
        rb = b_ref.shape[1] // n_heads
        xa = a_ref[:, pl.ds(h, ra, stride=n_heads), :].reshape(na, HEAD_DIM)
        xb = b_ref[:, pl.ds(h, rb, stride=n_heads), :].reshape(nb, HEAD_DIM)
        xn = n_ref[:, h * HEAD_DIM:(h + 1) * HEAD_DIM]
        parts = [xa, xb, xn]
        if tail:
            parts.append(jnp.zeros((tail, HEAD_DIM), F32))
        return jnp.concatenate(parts, axis=0).astype(BF16)

    outs = []
    for h in range(n_heads):
        kh = head_rows(ka_ref, kb_ref, kn_ref, h)
        vh = head_rows(va_ref, vb_ref, vn_ref, h)
        qh = jnp.concatenate(
            [q_ref[:, (gi * n_heads + h) * HEAD_DIM:(gi * n_heads + h + 1) * HEAD_DIM] for gi in range(n_groups)],
            axis=0).astype(BF16)
        s = lax.dot_general(qh, kh, nt, preferred_element_type=F32) + bias
        m = jnp.max(s, axis=-1, keepdims=True)
        p = jnp.exp(s - m)
        l = jnp.sum(p, axis=-1, keepdims=True)
        og = jnp.dot(p.astype(BF16), vh, preferred_element_type=F32) / l
        lse = m + jnp.log(l)
        ls = [lse[gi * dec_seq:(gi + 1) * dec_seq] for gi in range(n_groups)]
        mx = functools.reduce(jnp.maximum, ls)
        ws = [jnp.exp(x - mx) for x in ls]
        num = sum(w * og[gi * dec_seq:(gi + 1) * dec_seq] for gi, w in enumerate(ws))
        outs.append(num / sum(ws))
    o_ref[...] = jnp.concatenate(outs, axis=1)


def _decode_attention(q, k_new, v_new, cache_k, cache_v, *, q_row0, n_heads):
    batch, w_buf = cache_k.shape[:2]
    dec_seq = k_new.shape[0] // batch
    n_groups = len(DIL_GROUPS)
    comb, dense_from = _decode_rows(w_buf, dec_seq)
    assert dec_seq <= comb and comb % dec_seq == 0 and dec_seq % 8 == 0 and w_buf % comb == 0
    n_comb = dense_from // comb
    n_dense = (w_buf - dense_from) // comb
    key_pos = np.concatenate([
        (np.arange(n_comb)[:, None] * comb + np.arange(dec_seq)[None, :]).reshape(-1),
        dense_from + np.arange(w_buf - dense_from),
        w_buf + np.arange(dec_seq)])
    n_pad = -(-len(key_pos) // 128) * 128
    bias = jnp.asarray(_decode_bias(w_buf, dec_seq, key_pos, n_pad))
    ck = cache_k.reshape(batch, w_buf // comb, comb * n_heads, HEAD_DIM)
    cv = cache_v.reshape(batch, w_buf // comb, comb * n_heads, HEAD_DIM)
    tok = lambda width, row0=0: pl.BlockSpec((dec_seq, width), lambda b: (row0 // dec_seq + b, 0))
    comb_spec = pl.BlockSpec((None, n_comb, dec_seq * n_heads, HEAD_DIM), lambda b: (b, 0, 0, 0))
    dense_spec = pl.BlockSpec((None, n_dense, comb * n_heads, HEAD_DIM), lambda b: (b, n_comb // n_dense, 0, 0))
    assert n_comb % n_dense == 0
    return pl.pallas_call(
        functools.partial(_decode_attn_kernel, n_heads=n_heads, dec_seq=dec_seq, n_groups=n_groups, n_pad=n_pad),
        out_shape=jax.ShapeDtypeStruct((batch * dec_seq, n_heads * HEAD_DIM), F32),
        grid=(batch,),
        in_specs=[tok(q.shape[1], q_row0), tok(k_new.shape[1]), tok(v_new.shape[1]),
                  comb_spec, dense_spec, comb_spec, dense_spec,
                  pl.BlockSpec(bias.shape, lambda b: (0, 0))],
        out_specs=pl.BlockSpec((dec_seq, n_heads * HEAD_DIM), lambda b: (b, 0)),
        compiler_params=_params(("parallel",), 48),
        name="decode_attention",
    )(q, k_new, v_new, ck, ck, cv, cv, bias)


def _moe_and_ple(h, p_emb, layer, w, g_final, *, tm_moe, tm_ple, split=None):
    n, d = h.shape
    lanes = 128
    w_router = jnp.zeros((d, lanes), F32)
    w_router = w_router.at[:, :N_EXPERT_GROUPS].set(w["moe_w_rg"][layer])
    w_router = w_router.at[:, N_EXPERT_GROUPS:N_EXPERT_GROUPS + N_EXPERTS].set(w["moe_w_re"][layer])
    xn, info = _router(h, w["norm_ffn"][layer][None, :], w_router, tm=512)
    plan, pos0, pos1 = _route(info, tm=tm_moe, slab=d // SLAB_WIDTH)
    y_sorted = _experts(xn, plan, w["moe_w_gu"], w["moe_w_dn"], layer=layer, tm=tm_moe)
    ple = functools.partial(_ple, h, info, y_sorted, p_emb=p_emb, g_ple=w["norm_ple"][layer][None, :],
                            w_gate=w["ple_w_gate"][layer].astype(BF16), w_proj=w["ple_w_proj"][layer].astype(BF16),
                            g_final=g_final, tm=tm_ple)
    if split is None:
        return ple(pos0, pos1)
    return (ple(pos0[:split], pos1[:split], row0=0, n_rows=split),
            ple(pos0[split:], pos1[split:], row0=split, n_rows=n - split))


def kernel(x_prompt, x_sample, p_prompt, p_sample, state_gla, cache_k, cache_v, norm_mix, norm_ffn, norm_ple,
           norm_kv, norm_final, gla_w_in, gla_w_a2, gla_b_a, gla_g_out, gla_w_out, w_kv, dil_w_q, dil_w_out,
           moe_w_rg, moe_w_re, moe_w_gu, moe_w_dn, ple_w_gate, ple_w_proj):
    w = dict(norm_ffn=norm_ffn, norm_ple=norm_ple, moe_w_rg=moe_w_rg, moe_w_re=moe_w_re, moe_w_gu=moe_w_gu,
             moe_w_dn=moe_w_dn, ple_w_gate=ple_w_gate, ple_w_proj=ple_w_proj)
    bp, tp, d = x_prompt.shape
    bs, ts, _ = x_sample.shape
    n_p, n_s = bp * tp, bs * ts
    depth = p_prompt.shape[0]
    assert depth == 2 and state_gla.shape[0] == 1
    heads, dk, dv = state_gla.shape[2:]
    n_kv = cache_k.shape[2]
    past_len = cache_k.shape[1]
    qk_w, v_w = heads * dk, heads * dv
    lowrank = gla_w_a2.shape[1]

    x = jnp.concatenate([x_prompt.reshape(n_p, d), x_sample.reshape(n_s, d)], axis=0)
    p_emb = jnp.concatenate([p_prompt.reshape(depth, n_p, -1), p_sample.reshape(depth, n_s, -1)], axis=1)

    half = HEAD_DIM // 2
    inv = ROPE_THETA ** (-jnp.arange(half, dtype=F32) / half)
    pos = jnp.concatenate([jnp.tile(jnp.arange(tp), bp), jnp.tile(past_len + jnp.arange(ts), bs)]).astype(F32)
    ang = pos[:, None] * inv[None, :]
    rope = (jnp.concatenate([jnp.cos(ang), jnp.cos(ang)], axis=1),
            jnp.concatenate([-jnp.sin(ang), jnp.sin(ang)], axis=1))

    g_mix0 = norm_mix[0][None, :]
    w_in = gla_w_in[0].astype(BF16)
    main_w = 2 * qk_w + 2 * v_w
    tm_wide = 1024 if (n_p + n_s) % 1024 == 0 else 512
    z = _norm_linear(x, g_mix0, w_in, col_start=0, n_cols=main_w, tn=1024, tm=tm_wide)
    w_a1 = jnp.zeros((d, 128), BF16).at[:, :lowrank].set(w_in[:, main_w:])
    w_a2 = jnp.zeros((128, qk_w), F32).at[:lowrank].set(gla_w_a2[0])
    g = _gla_gate(x, g_mix0, w_a1, w_a2, gla_b_a[0][None, :], tm=512)
    c_p = int(np.gcd(tp, GLA_CHUNK))
    c_s = int(np.gcd(ts, GLA_CHUNK))
    gla = functools.partial(_gla_scan, z, g, heads=heads, dk=dk, dv=dv)
    o_p, sg_p = gla(None, row0=0, batch=bp, seq=tp, c=c_p, lanes=2 if bp % 2 == 0 else 1)
    o_s, sg_s = gla(state_gla[0], row0=n_p, batch=bs, seq=ts, c=c_s, lanes=2 if bs % 2 == 0 else 1)
    h = _gla_out(o_p, o_s, z, x, gla_g_out[0][None, :], gla_w_out[0].astype(BF16), heads=heads, tm=256)
    h = _moe_and_ple(h, p_emb[0], 0, w, None, tm_moe=64, tm_ple=128)

    w_kv_b = w_kv.astype(BF16)
    kv_w = n_kv * HEAD_DIM
    g_kv = norm_kv[None, :]
    kv_lin = functools.partial(_norm_linear, h, g_kv, w_kv_b, n_cols=kv_w, tn=kv_w, tm=512)
    k_p = kv_lin(col_start=0, rope=rope, row0=0, n_rows=n_p)
    k_s = kv_lin(col_start=0, rope=rope, row0=n_p, n_rows=n_s)
    v_p = kv_lin(col_start=kv_w, row0=0, n_rows=n_p)
    v_s = kv_lin(col_start=kv_w, row0=n_p, n_rows=n_s)
    q_all = _norm_linear(h, norm_mix[1][None, :], dil_w_q[0].astype(BF16), col_start=0,
                         n_cols=dil_w_q.shape[2], tn=kv_w, tm=tm_wide, rope=rope, scale=HEAD_DIM ** -0.5)

    a_p = _band_attention(q_all, k_p, v_p, batch=bp, seq=tp, n_heads=n_kv)
    a_s = _decode_attention(q_all, k_s, v_s, cache_k, cache_v, q_row0=n_p, n_heads=n_kv)
    h = _linear_res(a_p, a_s, dil_w_out[0].astype(BF16), h, tm=512)
    y_p, y_s = _moe_and_ple(h, p_emb[1], 1, w, norm_final[None, :], tm_moe=128, tm_ple=256, split=n_p)

    rows = min(past_len, tp)
    as_heads = lambda a, b, t: a.reshape(b, t, n_kv, HEAD_DIM)
    return (y_p.reshape(bp, tp, d), y_s.reshape(bs, ts, d), sg_p[None], sg_s[None],
            as_heads(k_p, bp, tp)[:, tp - rows:], as_heads(v_p, bp, tp)[:, tp - rows:],
            as_heads(k_s, bs, ts), as_heads(v_s, bs, ts))
```
